```python
import functools
import jax, jax.numpy as jnp
from jax import lax
import numpy as np

D_MODEL = 1024
BATCH = 2
SEQ = 8192
DEPTH = 4
DEC_BATCH = 32
DEC_SEQ = 8
PAST_LEN = 8192
PAGE_SIZE = 128

N_EVEN = (DEPTH + 1) // 2
N_ODD = DEPTH // 2
H_A = 8
HD_A = 64
W_A = H_A * HD_A
H_B = 8
BW_B = 64
W_B = H_B * BW_B
CONV_B = 4
C_LRU = 8.0
H_C = 8
CW_C = 128
W_C = H_C * CW_C
CHUNK = 128
D_FF = 3 * D_MODEL
CONV_F = 3
Q_BLOCK = 128
EPS = 1e-6
D_IN_EVEN = 3 * W_A + 2 * W_B

kernel_name = "stickbreak_rglru_chunkmlp_convffn_step"


def rmsnorm(x, g):
    xf = x.astype(jnp.float32)
    y = xf * lax.rsqrt(jnp.mean(xf * xf, axis=-1, keepdims=True) + EPS)
    return (y * g.astype(jnp.float32)).astype(x.dtype)


def causal_dwconv(x, buf, w, b):
    width = w.shape[0]
    t = x.shape[1]
    xp = jnp.concatenate([buf.astype(x.dtype), x], axis=1)
    y = b
    for j in range(width):
        y = y + w[j] * xp[:, j:j + t]
    return y, xp[:, xp.shape[1] - (width - 1):]


def sb_attend(q, k, v, bias, q_pos, k_pos):
    z = (jnp.einsum('bqhd,bkhd->bhqk', q, k).astype(jnp.float32) * (HD_A ** -0.5)
         + bias.astype(jnp.float32)[None, :, None, None])
    causal = k_pos[None, :] < q_pos[:, None]
    log_keep = jnp.where(causal, jax.nn.log_sigmoid(-z), 0.0)
    log_after = lax.cumsum(log_keep, axis=3, reverse=True) - log_keep
    w = jnp.where(causal, jnp.exp(jax.nn.log_sigmoid(z) + log_after), 0.0)
    return jnp.einsum('bhqk,bkhd->bqhd', w.astype(v.dtype), v)


def sb_prompt(q, k, v, bias):
    b, t, h, d = q.shape
    nb = t // Q_BLOCK
    qb = jnp.moveaxis(q.reshape(b, nb, Q_BLOCK, h, d), 1, 0)
    qpos = jnp.arange(t, dtype=jnp.int32).reshape(nb, Q_BLOCK)
    kpos = jnp.arange(t, dtype=jnp.int32)
    ob = lax.map(lambda a: sb_attend(a[0], k, v, bias, a[1], kpos), (qb, qpos))
    return jnp.moveaxis(ob, 0, 1).reshape(b, t, h, d)


def sb_cached(q, k, v, bias, k_past, v_past):
    past = k_past.shape[1]
    t = q.shape[1]
    k_all = jnp.concatenate([k_past.astype(k.dtype), k], axis=1)
    v_all = jnp.concatenate([v_past.astype(v.dtype), v], axis=1)
    q_pos = past + jnp.arange(t, dtype=jnp.int32)
    k_pos = jnp.arange(past + t, dtype=jnp.int32)
    return sb_attend(q, k_all, v_all, bias, q_pos, k_pos)


def _lin_combine(prev, nxt):
    a1, b1 = prev
    a2, b2 = nxt
    return a1 * a2, a2 * b1 + b2


def rglru(xc, h0, w_r, b_r, w_i, b_i, lam):
    b, t, _ = xc.shape
    xf = xc.astype(jnp.float32).reshape(b, t, H_B, BW_B)
    r = jax.nn.sigmoid(jnp.einsum('bthc,hcd->bthd', xf, w_r.astype(jnp.float32)) + b_r)
    i = jax.nn.sigmoid(jnp.einsum('bthc,hcd->bthd', xf, w_i.astype(jnp.float32)) + b_i)
    log_a = -C_LRU * r * jax.nn.softplus(-lam.astype(jnp.float32))
    a = jnp.exp(log_a).reshape(b, t, W_B)
    u = (jnp.sqrt(-jnp.expm1(2.0 * log_a)) * (i * xf)).reshape(b, t, W_B)
    u = u.at[:, 0].add(a[:, 0] * h0.astype(jnp.float32))
    _, h = lax.associative_scan(_lin_combine, (a, u), axis=1)
    return h.astype(xc.dtype), h[:, -1].astype(xc.dtype)


def even_mixer(xn, h0, conv_buf, attend, w_in, gq, gk, sb_bias, w_conv, b_conv, w_r, b_r, w_i, b_i, lam, w_out):
    b, t, _ = xn.shape
    proj = xn @ w_in
    q, k, v, xb, gb = jnp.split(proj, [W_A, 2 * W_A, 3 * W_A, 3 * W_A + W_B], axis=-1)
    q = rmsnorm(q.reshape(b, t, H_A, HD_A), gq)
    k = rmsnorm(k.reshape(b, t, H_A, HD_A), gk)
    v = v.reshape(b, t, H_A, HD_A)
    o_a = attend(q, k, v, sb_bias).reshape(b, t, W_A)
    xc, new_buf = causal_dwconv(xb, conv_buf, w_conv, b_conv)
    h, h_last = rglru(xc, h0, w_r, b_r, w_i, b_i, lam)
    o_b = h * jax.nn.gelu(gb)
    out = jnp.concatenate([o_a, o_b], axis=-1) @ w_out
    return out, k, v, h_last, new_buf


def chunk_mix(v, w_s, b_s):
    b, t, h, c = v.shape
    nc = -(-t // CHUNK)
    vp = jnp.pad(v, ((0, 0), (0, nc * CHUNK - t), (0, 0), (0, 0))).reshape(b, nc, CHUNK, h, c)
    w = w_s * jnp.tril(jnp.ones((CHUNK, CHUNK), w_s.dtype))
    o = jnp.einsum('hts,bnshc->bnthc', w, vp) + jnp.swapaxes(b_s, 0, 1)[:, :, None]
    return o.reshape(b, nc * CHUNK, h, c)[:, :t]


def odd_mixer(xn, w_in, gv, w_s, b_s, w_out):
    b, t, _ = xn.shape
    z = jax.nn.gelu(xn @ w_in)
    u, v = jnp.split(z, 2, axis=-1)
    v = rmsnorm(v.reshape(b, t, H_C, CW_C), gv.reshape(H_C, CW_C))
    vm = chunk_mix(v, w_s, b_s).reshape(b, t, W_C)
    return (u * vm) @ w_out, v.reshape(b, t, W_C)


def conv_ffn(xn, buf, w_g, w_u, w_c, b_c, w_d):
    g = xn @ w_g
    gc, new_buf = causal_dwconv(g, buf, w_c, b_c)
    return (jax.nn.gelu(gc) * (xn @ w_u)) @ w_d, new_buf


def setup_inputs(seed: int = 0) -> dict:
    key = jax.random.key(seed)
    ks = iter(jax.random.split(key, 48))
    f32 = jnp.float32

    def nrm(shape, scale):
        return jax.random.normal(next(ks), shape, f32) * scale

    n_pages = PAST_LEN // PAGE_SIZE
    n_used = DEC_BATCH * n_pages
    n_pool = (n_used * 5) // 4
    perm = jax.random.permutation(next(ks), n_pool)
    page_table = perm[:n_used].reshape(DEC_BATCH, n_pages).astype(jnp.int32)

    u = jax.random.uniform(next(ks), (N_EVEN, H_B, BW_B), f32, minval=0.9, maxval=0.999)
    sig = u ** (1.0 / C_LRU)
    lru_lambda = jnp.log(sig) - jnp.log1p(-sig)
    sb_bias = jax.random.uniform(next(ks), (N_EVEN, H_A), f32, minval=-8.0, maxval=-6.0)

    return {
        "x_prompt": nrm((BATCH, SEQ, D_MODEL), 1.0),
        "x_sample": nrm((DEC_BATCH, DEC_SEQ, D_MODEL), 1.0),
        "cache_k": nrm((N_EVEN, n_pool, PAGE_SIZE, H_A, HD_A), 1.0),
        "cache_v": nrm((N_EVEN, n_pool, PAGE_SIZE, H_A, HD_A), 1.0),
        "state_lru_h": nrm((N_EVEN, DEC_BATCH, W_B), 0.5),
        "state_lru_conv": nrm((N_EVEN, DEC_BATCH, CONV_B - 1, W_B), 1.0),
        "state_ffn_conv": nrm((DEPTH, DEC_BATCH, CONV_F - 1, D_FF), 1.0),
        "page_table": page_table,
        "g_mix": 1.0 + nrm((DEPTH, D_MODEL), 0.02),
        "g_ffn": 1.0 + nrm((DEPTH, D_MODEL), 0.02),
        "w_in_even": nrm((N_EVEN, D_MODEL, D_IN_EVEN), D_MODEL ** -0.5),
        "g_q": 1.0 + nrm((N_EVEN, HD_A), 0.02),
        "g_k": 1.0 + nrm((N_EVEN, HD_A), 0.02),
        "sb_bias": sb_bias,
        "w_conv_lru": nrm((N_EVEN, CONV_B, W_B), CONV_B ** -0.5),
        "b_conv_lru": nrm((N_EVEN, W_B), 0.02),
        "w_rgate": nrm((N_EVEN, H_B, BW_B, BW_B), BW_B ** -0.5),
        "b_rgate": nrm((N_EVEN, H_B, BW_B), 0.02),
        "w_igate": nrm((N_EVEN, H_B, BW_B, BW_B), BW_B ** -0.5),
        "b_igate": nrm((N_EVEN, H_B, BW_B), 0.02),
        "lru_lambda": lru_lambda,
        "w_out_even": nrm((N_EVEN, W_A + W_B, D_MODEL), (W_A + W_B) ** -0.5),
        "w_in_odd": nrm((N_ODD, D_MODEL, 2 * W_C), D_MODEL ** -0.5),
        "g_v": 1.0 + nrm((N_ODD, W_C), 0.02),
        "w_spatial": nrm((N_ODD, H_C, CHUNK, CHUNK), CHUNK ** -0.5),
        "b_spatial": 1.0 + nrm((N_ODD, H_C, CHUNK), 0.1),
        "w_out_odd": nrm((N_ODD, W_C, D_MODEL), W_C ** -0.5),
        "w_gate": nrm((DEPTH, D_MODEL, D_FF), D_MODEL ** -0.5),
        "w_up": nrm((DEPTH, D_MODEL, D_FF), D_MODEL ** -0.5),
        "w_ffn_conv": nrm((DEPTH, CONV_F, D_FF), CONV_F ** -0.5),
        "b_ffn_conv": nrm((DEPTH, D_FF), 0.02),
        "w_down": nrm((DEPTH, D_FF, D_MODEL), D_FF ** -0.5),
    }


def reference(x_prompt, x_sample, cache_k, cache_v, state_lru_h, state_lru_conv, state_ffn_conv, page_table,
              g_mix, g_ffn, w_in_even, g_q, g_k, sb_bias, w_conv_lru, b_conv_lru, w_rgate, b_rgate, w_igate, b_igate,
              lru_lambda, w_out_even, w_in_odd, g_v, w_spatial, b_spatial, w_out_odd,
              w_gate, w_up, w_ffn_conv, b_ffn_conv, w_down):
    bp, tp = x_prompt.shape[0], x_prompt.shape[1]
    bs = x_sample.shape[0]
    n_pages = page_table.shape[1]
    past = n_pages * PAGE_SIZE
    yp, ys = x_prompt, x_sample
    h0_p = jnp.zeros((bp, W_B), x_prompt.dtype)
    cbuf_p = jnp.zeros((bp, CONV_B - 1, W_B), x_prompt.dtype)
    fbuf_p = jnp.zeros((bp, CONV_F - 1, D_FF), x_prompt.dtype)
    last_chunk_start = ((tp - 1) // CHUNK) * CHUNK
    kp_l, vp_l, ks_l, vs_l, hp_l, hs_l, cp_l, cs_l = [], [], [], [], [], [], [], []
    chp_l, chs_l, fp_l, fs_l = [], [], [], []
    for l in range(DEPTH):
        if l % 2 == 0:
            e = l // 2
            ev = (w_in_even[e], g_q[e], g_k[e], sb_bias[e], w_conv_lru[e], b_conv_lru[e], w_rgate[e], b_rgate[e],
                  w_igate[e], b_igate[e], lru_lambda[e], w_out_even[e])
            k_past = cache_k[e][page_table].reshape(bs, past, H_A, HD_A)
            v_past = cache_v[e][page_table].reshape(bs, past, H_A, HD_A)
            attend_s = functools.partial(sb_cached, k_past=k_past, v_past=v_past)
            m_p, k_p, v_p, h_p, c_p = even_mixer(rmsnorm(yp, g_mix[l]), h0_p, cbuf_p, sb_prompt, *ev)
            m_s, k_s, v_s, h_s, c_s = even_mixer(rmsnorm(ys, g_mix[l]), state_lru_h[e], state_lru_conv[e],
                                                 attend_s, *ev)
            kp_l.append(k_p); vp_l.append(v_p); ks_l.append(k_s); vs_l.append(v_s)
            hp_l.append(h_p); hs_l.append(h_s); cp_l.append(c_p); cs_l.append(c_s)
        else:
            o = l // 2
            od = (w_in_odd[o], g_v[o], w_spatial[o], b_spatial[o], w_out_odd[o])
            m_p, vr_p = odd_mixer(rmsnorm(yp, g_mix[l]), *od)
            m_s, vr_s = odd_mixer(rmsnorm(ys, g_mix[l]), *od)
            chp_l.append(vr_p[:, last_chunk_start:]); chs_l.append(vr_s)
        yp = yp + m_p
        ys = ys + m_s
        fw = (w_gate[l], w_up[l], w_ffn_conv[l], b_ffn_conv[l], w_down[l])
        f_p, fb_p = conv_ffn(rmsnorm(yp, g_ffn[l]), fbuf_p, *fw)
        f_s, fb_s = conv_ffn(rmsnorm(ys, g_ffn[l]), state_ffn_conv[l], *fw)
        yp = yp + f_p
        ys = ys + f_s
        fp_l.append(fb_p); fs_l.append(fb_s)
    return (yp, ys,
            jnp.stack(kp_l), jnp.stack(vp_l), jnp.stack(ks_l), jnp.stack(vs_l),
            jnp.stack(hp_l), jnp.stack(hs_l), jnp.stack(cp_l), jnp.stack(cs_l),
            jnp.stack(chp_l), jnp.stack(chs_l), jnp.stack(fp_l), jnp.stack(fs_l))
```

```python
import functools
import math

import numpy as np
import jax
import jax.numpy as jnp
from jax import lax
from jax.experimental import pallas as pl
from jax.experimental.pallas import tpu as pltpu

F32 = jnp.float32
BF16 = jnp.bfloat16

D_MODEL = 1024
H_A, HD_A = 8, 64
W_A = H_A * HD_A
H_B, BW_B = 8, 64
W_B = H_B * BW_B
CONV_B = 4
C_LRU = 8.0
H_C, CW_C = 8, 128
W_C = H_C * CW_C
CHUNK = 128
D_FF = 3 * D_MODEL
CONV_F = 3
PAGE = 128
EPS = 1e-6
D_IN_EVEN = 3 * W_A + 2 * W_B
LOG2E = math.log2(math.e)

LANES = 128
SUBLANES = 8
VMEM_LIMIT = 56 * 1024 * 1024


def _cparams(sem):
    return pltpu.CompilerParams(dimension_semantics=sem, vmem_limit_bytes=VMEM_LIMIT)


def _rms(x, g):
    ms = jnp.mean(x * x, axis=-1, keepdims=True)
    return x * lax.rsqrt(ms + EPS) * g


def _dot(a, b):
    return jnp.dot(a, b, preferred_element_type=F32)


def _dot_nt(a, b):
    return lax.dot_general(a, b, (((1,), (1,)), ((), ())), preferred_element_type=F32)


def _even_in_kernel(x_ref, g_ref, w_ref, pm_ref, gq_ref, gk_ref,
                    q_ref, k_ref, v_ref, kb_ref, vb_ref, xb_ref, gb_ref):
    xn = _rms(x_ref[...], g_ref[...]).astype(BF16)
    p = _dot(xn, w_ref[...])
    pm = pm_ref[...]

    def head_norm(t, g):
        sq = t * t
        hi = sq.astype(BF16)
        lo = (sq - hi.astype(F32)).astype(BF16)
        ms = _dot(hi, pm) + _dot(lo, pm)
        return t * lax.rsqrt(ms + EPS) * g

    q = head_norm(p[:, 0:W_A], gq_ref[...])
    k = head_norm(p[:, W_A:2 * W_A], gk_ref[...])
    v = p[:, 2 * W_A:3 * W_A]
    q_ref[...] = q.astype(BF16)
    k_ref[...] = k
    v_ref[...] = v
    kb_ref[...] = k.astype(BF16)
    vb_ref[...] = v.astype(BF16)
    xb_ref[...] = p[:, 3 * W_A:3 * W_A + W_B]
    gb_ref[...] = p[:, 3 * W_A + W_B:]


def _even_in(x, g, w, pm, gq, gk, tm):
    m = x.shape[0]
    row = lambda i: (i, 0)
    fixed = lambda i: (0, 0)
    f32o = jax.ShapeDtypeStruct((m, W_A), F32)
    bfo = jax.ShapeDtypeStruct((m, W_A), BF16)
    blk = pl.BlockSpec((tm, W_A), row)
    return pl.pallas_call(
        _even_in_kernel,
        grid=(m // tm,),
        in_specs=[pl.BlockSpec((tm, D_MODEL), row), pl.BlockSpec((1, D_MODEL), fixed),
                  pl.BlockSpec((D_MODEL, D_IN_EVEN), fixed), pl.BlockSpec((W_A, W_A), fixed),
                  pl.BlockSpec((1, W_A), fixed), pl.BlockSpec((1, W_A), fixed)],
        out_specs=[blk] * 7,
        out_shape=[bfo, f32o, f32o, bfo, bfo, f32o, f32o],
        compiler_params=_cparams(("parallel",)),
        name="even_in",
    )(x, g, w, pm, gq, gk)


def _sb_weights(z, tri, carry, mask):
    sp = jnp.maximum(z, 0.0) + jnp.log2(1.0 + jnp.exp2(-jnp.abs(z)))
    if mask is not None:
        sp = jnp.where(mask, sp, 0.0)
    spb = sp.astype(BF16)
    c = _dot(spb, tri)
    w = jnp.exp2(z - sp - c - carry)
    if mask is not None:
        w = jnp.where(mask, w, 0.0)
    new_carry = carry + c[:, 0:1] + spb[:, 0:1].astype(F32)
    return w.astype(BF16), new_carry


def _attn_prompt_kernel(bias_ref, q_ref, k_ref, v_ref, tri_ref, o_ref, *, tq):
    pair = pl.program_id(1)
    i = pl.program_id(2)
    q = q_ref[0]
    tri = tri_ref[...]
    lane = lax.broadcasted_iota(jnp.int32, (tq, LANES), 1)
    first = lane < HD_A
    zero = jnp.zeros_like(q)
    qm = (jnp.where(first, q, zero), jnp.where(first, zero, q))
    b2 = (bias_ref[2 * pair], bias_ref[2 * pair + 1])
    rr = lax.broadcasted_iota(jnp.int32, (tq, tq), 0)
    cc = lax.broadcasted_iota(jnp.int32, (tq, tq), 1)
    causal = cc < rr

    def block(j, state, mask):
        start = pl.multiple_of(j * tq, tq)
        kj = k_ref[0, pl.ds(start, tq), :]
        vj = v_ref[0, pl.ds(start, tq), :]
        out = []
        for hh in range(2):
            carry, acc = state[hh]
            z = _dot_nt(qm[hh], kj) + b2[hh]
            w, carry = _sb_weights(z, tri, carry, mask)
            out.append((carry, acc + _dot(w, vj)))
        return tuple(out)

    init = tuple((jnp.zeros((tq, 1), F32), jnp.zeros((tq, LANES), F32)) for _ in range(2))
    state = block(i, init, causal)
    state = lax.fori_loop(0, i, lambda n, s: block(i - 1 - n, s, None), state)
    o_ref[0] = jnp.where(first, state[0][1], state[1][1]).astype(o_ref.dtype)


def _attn_prompt(q, k, v, bias2, tri, tq):
    b, t, _ = q.shape
    npair = W_A // LANES
    return pl.pallas_call(
        functools.partial(_attn_prompt_kernel, tq=tq),
        grid=(b, npair, t // tq),
        in_specs=[pl.BlockSpec(memory_space=pltpu.SMEM),
                  pl.BlockSpec((1, tq, LANES), lambda bi, p, i: (bi, i, p)),
                  pl.BlockSpec((1, t, LANES), lambda bi, p, i: (bi, 0, p)),
                  pl.BlockSpec((1, t, LANES), lambda bi, p, i: (bi, 0, p)),
                  pl.BlockSpec((tq, tq), lambda bi, p, i: (0, 0))],
        out_specs=pl.BlockSpec((1, tq, LANES), lambda bi, p, i: (bi, i, p)),
        out_shape=jax.ShapeDtypeStruct((b, t, W_A), BF16),
        compiler_params=_cparams(("parallel", "parallel", "arbitrary")),
        name="attn_prompt",
    )(bias2, q, k, v, tri)


def _attn_sample_kernel(pt_ref, qbd_ref, brow_ref, kn_ref, vn_ref, tri_ref, *rest, n_group, pairs):
    del pt_ref
    pages = rest[:2 * n_group]
    o_ref, acc_ref, car_ref = rest[2 * n_group:]
    p = pl.program_id(1)
    qbd = qbd_ref[0]
    brow = brow_ref[...]
    tri = tri_ref[...]
    rows = H_A * SUBLANES

    @pl.when(p == 0)
    def _():
        z = _dot(qbd, kn_ref[0]) + brow
        rr = lax.broadcasted_iota(jnp.int32, (rows, PAGE), 0) % SUBLANES
        cc = lax.broadcasted_iota(jnp.int32, (rows, PAGE), 1)
        w, carry = _sb_weights(z, tri[0:PAGE, 0:PAGE], jnp.zeros((rows, 1), F32), cc < rr)
        acc_ref[...] = _dot_nt(w, vn_ref[0])
        car_ref[...] = carry

    carry = car_ref[...]
    acc = acc_ref[...]
    for s in reversed(range(pairs)):
        k2 = jnp.concatenate([pages[2 * s][0, 0], pages[2 * s + 1][0, 0]], axis=1).astype(BF16)
        v2 = jnp.concatenate([pages[n_group + 2 * s][0, 0], pages[n_group + 2 * s + 1][0, 0]],
                             axis=1).astype(BF16)
        z = _dot(qbd, k2) + brow
        w, carry = _sb_weights(z, tri, carry, None)
        acc = acc + _dot_nt(w, v2)
    car_ref[...] = carry
    acc_ref[...] = acc

    @pl.when(p == pl.num_programs(1) - 1)
    def _():
        lane_head = lax.broadcasted_iota(jnp.int32, (SUBLANES, W_A), 1) // HD_A
        out = jnp.zeros((SUBLANES, W_A), F32)
        for h in range(H_A):
            out = jnp.where(lane_head == h, acc[h * SUBLANES:(h + 1) * SUBLANES, :], out)
        o_ref[0] = out.astype(o_ref.dtype)


def _attn_sample(page_table, qbd, brow, knew, vnew, tri, cache_k, cache_v, e, n_group):
    bs, n_pages = page_table.shape
    steps = n_pages // n_group
    rows = H_A * SUBLANES

    def page_spec(i):
        return pl.BlockSpec(
            (1, 1, W_A, PAGE),
            lambda b, p, pt, i=i: (e, pt[b, (steps - 1 - p) * n_group + i], 0, 0))

    grid_spec = pltpu.PrefetchScalarGridSpec(
        num_scalar_prefetch=1,
        grid=(bs, steps),
        in_specs=[pl.BlockSpec((1, rows, W_A), lambda b, p, pt: (b, 0, 0)),
                  pl.BlockSpec((rows, 1), lambda b, p, pt: (0, 0)),
                  pl.BlockSpec((1, W_A, PAGE), lambda b, p, pt: (b, 0, 0)),
                  pl.BlockSpec((1, W_A, PAGE), lambda b, p, pt: (b, 0, 0)),
                  pl.BlockSpec((2 * PAGE, 2 * PAGE), lambda b, p, pt: (0, 0))]
                 + [page_spec(i) for i in range(n_group)] * 2,
        out_specs=pl.BlockSpec((1, SUBLANES, W_A), lambda b, p, pt: (b, 0, 0)),
        scratch_shapes=[pltpu.VMEM((rows, W_A), F32), pltpu.VMEM((rows, 1), F32)],
    )
    return pl.pallas_call(
        functools.partial(_attn_sample_kernel, n_group=n_group, pairs=n_group // 2),
        grid_spec=grid_spec,
        out_shape=jax.ShapeDtypeStruct((bs, SUBLANES, W_A), BF16),
        compiler_params=_cparams(("parallel", "arbitrary")),
        name="attn_sample",
    )(page_table, qbd, brow, knew, vnew, tri, *([cache_k] * n_group), *([cache_v] * n_group))


def _lru_kernel(xb_ref, gb_ref, h0_ref, cb_ref, wc_ref, bc_ref, wr_ref, br_ref, wi_ref, bi_ref, lam_ref,
                ob_ref, hl_ref, xs_ref, hc_ref, *, tt):
    t = pl.program_id(1)

    @pl.when(t == 0)
    def _():
        xs_ref[0:SUBLANES, :] = cb_ref[0]
        hc_ref[...] = h0_ref[0]

    x = xb_ref[0]
    xs_ref[SUBLANES:SUBLANES + tt, :] = x
    wc = wc_ref[...]
    xc = (bc_ref[...] + wc[3:4] * x + wc[2:3] * xs_ref[SUBLANES - 1:SUBLANES - 1 + tt, :]
          + wc[1:2] * xs_ref[SUBLANES - 2:SUBLANES - 2 + tt, :]
          + wc[0:1] * xs_ref[SUBLANES - 3:SUBLANES - 3 + tt, :])
    xs_ref[0:SUBLANES, :] = x[tt - SUBLANES:tt]

    xcb = xc.astype(BF16)
    r = jax.nn.sigmoid(_dot(xcb, wr_ref[...]) + br_ref[...])
    ig = jax.nn.sigmoid(_dot(xcb, wi_ref[...]) + bi_ref[...])
    nl = -lam_ref[...]
    sp_lam = jnp.maximum(nl, 0.0) + jnp.log1p(jnp.exp(-jnp.abs(nl)))
    log_a = -C_LRU * r * sp_lam
    a = jnp.exp(log_a)
    u = jnp.sqrt((1.0 + a * a) * jnp.tanh(-log_a)) * (ig * xc)
    row = lax.broadcasted_iota(jnp.int32, (tt, W_B), 0)
    u = u + jnp.where(row == 0, a * hc_ref[...], 0.0)

    d = 1
    while d < tt:
        keep = row >= d
        a_sh = jnp.where(keep, pltpu.roll(a, d, 0), 1.0)
        u_sh = jnp.where(keep, pltpu.roll(u, d, 0), 0.0)
        u = a * u_sh + u
        a = a * a_sh
        d *= 2
    h = u
    hc_ref[...] = h[tt - 1:tt]
    hl_ref[0] = h[tt - SUBLANES:tt]
    ob_ref[0] = (h * jax.nn.gelu(gb_ref[0])).astype(ob_ref.dtype)


def _lru(xb, gb, h0, cbuf8, wc, bc, wr, br, wi, bi, lam, tt):
    b, t, _ = xb.shape
    seq = lambda bi, ti: (bi, ti, 0)
    per_b = lambda bi, ti: (bi, 0, 0)
    fixed = lambda bi, ti: (0, 0)
    return pl.pallas_call(
        functools.partial(_lru_kernel, tt=tt),
        grid=(b, t // tt),
        in_specs=[pl.BlockSpec((1, tt, W_B), seq), pl.BlockSpec((1, tt, W_B), seq),
                  pl.BlockSpec((1, 1, W_B), per_b), pl.BlockSpec((1, SUBLANES, W_B), per_b),
                  pl.BlockSpec((CONV_B, W_B), fixed), pl.BlockSpec((1, W_B), fixed),
                  pl.BlockSpec((W_B, W_B), fixed), pl.BlockSpec((1, W_B), fixed),
                  pl.BlockSpec((W_B, W_B), fixed), pl.BlockSpec((1, W_B), fixed),
                  pl.BlockSpec((1, W_B), fixed)],
        out_specs=[pl.BlockSpec((1, tt, W_B), seq), pl.BlockSpec((1, SUBLANES, W_B), per_b)],
        out_shape=[jax.ShapeDtypeStruct((b, t, W_B), BF16),
                   jax.ShapeDtypeStruct((b, SUBLANES, W_B), F32)],
        scratch_shapes=[pltpu.VMEM((tt + SUBLANES, W_B), F32), pltpu.VMEM((1, W_B), F32)],
        compiler_params=_cparams(("parallel", "arbitrary")),
        name="rglru",
    )(xb, gb, h0, cbuf8, wc, bc, wr, br, wi, bi, lam)


def _even_out_kernel(x_ref, oa_ref, ob_ref, w_ref, y_ref):
    y_ref[...] = (x_ref[...] + _dot(oa_ref[...], w_ref[0:W_A, :])
                  + _dot(ob_ref[...], w_ref[W_A:W_A + W_B, :]))


def _even_out(x, oa, ob, w, tm):
    m = x.shape[0]
    row = lambda i: (i, 0)
    return pl.pallas_call(
        _even_out_kernel,
        grid=(m // tm,),
        in_specs=[pl.BlockSpec((tm, D_MODEL), row), pl.BlockSpec((tm, W_A), row),
                  pl.BlockSpec((tm, W_B), row), pl.BlockSpec((W_A + W_B, D_MODEL), lambda i: (0, 0))],
        out_specs=pl.BlockSpec((tm, D_MODEL), row),
        out_shape=jax.ShapeDtypeStruct((m, D_MODEL), F32),
        compiler_params=_cparams(("parallel",)),
        name="even_out",
    )(x, oa, ob, w)


def _odd_kernel(x_ref, g_ref, win_ref, gv_ref, wmix_ref, bmix_ref, wout_ref, y_ref, vn_ref, um_ref,
                *, tm, rblk):
    x = x_ref[...]
    xn = _rms(x, g_ref[...]).astype(BF16)
    z = jax.nn.gelu(_dot(xn, win_ref[...]))
    for h in range(H_C):
        lo, hi = h * CW_C, (h + 1) * CW_C
        vh = z[:, W_C + lo:W_C + hi]
        ms = jnp.mean(vh * vh, axis=-1, keepdims=True)
        vnh = vh * lax.rsqrt(ms + EPS) * gv_ref[:, lo:hi]
        vn_ref[:, lo:hi] = vnh
        vnb = vnh.astype(BF16)
        wm = wmix_ref[h]
        for c in range(tm // rblk):
            r0, r1 = c * rblk, (c + 1) * rblk
            mixed = _dot(wm, vnb[r0:r1]) + bmix_ref[:, lo:hi]
            um_ref[r0:r1, lo:hi] = (z[r0:r1, lo:hi] * mixed).astype(BF16)
    y_ref[...] = x + _dot(um_ref[...], wout_ref[...])


def _odd_mixer(x, g, win, gv, wmix, bmix, wout, tm, rblk):
    m = x.shape[0]
    row = lambda i: (i, 0)
    fixed = lambda i: (0, 0)
    return pl.pallas_call(
        functools.partial(_odd_kernel, tm=tm, rblk=rblk),
        grid=(m // tm,),
        in_specs=[pl.BlockSpec((tm, D_MODEL), row), pl.BlockSpec((1, D_MODEL), fixed),
                  pl.BlockSpec((D_MODEL, 2 * W_C), fixed), pl.BlockSpec((1, W_C), fixed),
                  pl.BlockSpec((H_C, rblk, rblk), lambda i: (0, 0, 0)),
                  pl.BlockSpec((rblk, W_C), fixed), pl.BlockSpec((W_C, D_MODEL), fixed)],
        out_specs=[pl.BlockSpec((tm, D_MODEL), row), pl.BlockSpec((tm, W_C), row)],
        out_shape=[jax.ShapeDtypeStruct((m, D_MODEL), F32), jax.ShapeDtypeStruct((m, W_C), F32)],
        scratch_shapes=[pltpu.VMEM((tm, W_C), BF16)],
        compiler_params=_cparams(("parallel",)),
        name="odd_mixer",
    )(x, g, win, gv, wmix, bmix, wout)


def _ffn_kernel(x_ref, g_ref, init_ref, wg_ref, wu_ref, wc_ref, bc_ref, wd_ref,
                y_ref, gt_ref, xn_ref, gs_ref, car_ref, *, tm, tiles_per_seq, short_seq):
    m = pl.program_id(0)
    j = pl.program_id(1)

    @pl.when(j == 0)
    def _():
        xn_ref[...] = _rms(x_ref[...], g_ref[...]).astype(BF16)

    xn = xn_ref[...]
    g = _dot(xn, wg_ref[...])
    u = _dot(xn, wu_ref[...])
    wc = wc_ref[...]
    if short_seq:
        hist = init_ref[...]
        r8 = lax.broadcasted_iota(jnp.int32, g.shape, 0) % SUBLANES
        p1 = jnp.where(r8 == 0, pltpu.roll(hist, tm - 1, 0), pltpu.roll(g, 1, 0))
        p2 = jnp.where(r8 < 2, hist, pltpu.roll(g, 2, 0))
        gt_ref[...] = g
    else:
        @pl.when(m % tiles_per_seq == 0)
        def _():
            car_ref[j] = init_ref[0]

        gs_ref[0:SUBLANES, :] = car_ref[j]
        gs_ref[SUBLANES:SUBLANES + tm, :] = g
        p1 = gs_ref[SUBLANES - 1:SUBLANES - 1 + tm, :]
        p2 = gs_ref[SUBLANES - 2:SUBLANES - 2 + tm, :]
        tail = g[tm - SUBLANES:tm]
        car_ref[j] = tail
        gt_ref[0] = tail
    gc = bc_ref[...] + wc[2:3] * g + wc[1:2] * p1 + wc[0:1] * p2
    hid = (jax.nn.gelu(gc) * u).astype(BF16)
    contrib = _dot(hid, wd_ref[...])

    @pl.when(j == 0)
    def _():
        y_ref[...] = x_ref[...] + contrib

    @pl.when(j > 0)
    def _():
        y_ref[...] += contrib


def _ffn(x, g, init, wg, wu, wc, bc, wd, tm, tf, seq_len):
    m = x.shape[0]
    nff = D_FF // tf
    short_seq = seq_len < tm
    if short_seq:
        assert seq_len == SUBLANES and m == tm
        tiles_per_seq = 1
        init_spec = pl.BlockSpec((tm, tf), lambda i, j: (i, j))
        gt_spec = pl.BlockSpec((tm, tf), lambda i, j: (i, j))
        gt_shape = jax.ShapeDtypeStruct((m, D_FF), F32)
    else:
        assert seq_len % tm == 0
        tiles_per_seq = seq_len // tm
        init_spec = pl.BlockSpec((1, SUBLANES, tf), lambda i, j: (i // tiles_per_seq, 0, j))
        gt_spec = pl.BlockSpec((1, SUBLANES, tf), lambda i, j: (i, 0, j))
        gt_shape = jax.ShapeDtypeStruct((m // tm, SUBLANES, D_FF), F32)
    return pl.pallas_call(
        functools.partial(_ffn_kernel, tm=tm, tiles_per_seq=tiles_per_seq, short_seq=short_seq),
        grid=(m // tm, nff),
        in_specs=[pl.BlockSpec((tm, D_MODEL), lambda i, j: (i, 0)),
                  pl.BlockSpec((1, D_MODEL), lambda i, j: (0, 0)),
                  init_spec,
                  pl.BlockSpec((D_MODEL, tf), lambda i, j: (0, j)),
                  pl.BlockSpec((D_MODEL, tf), lambda i, j: (0, j)),
                  pl.BlockSpec((CONV_F, tf), lambda i, j: (0, j)),
                  pl.BlockSpec((1, tf), lambda i, j: (0, j)),
                  pl.BlockSpec((tf, D_MODEL), lambda i, j: (j, 0))],
        out_specs=[pl.BlockSpec((tm, D_MODEL), lambda i, j: (i, 0)), gt_spec],
        out_shape=[jax.ShapeDtypeStruct((m, D_MODEL), F32), gt_shape],
        scratch_shapes=[pltpu.VMEM((tm, D_MODEL), BF16), pltpu.VMEM((tm + SUBLANES, tf), F32),
                        pltpu.VMEM((nff, SUBLANES, tf), F32)],
        compiler_params=_cparams(("arbitrary", "arbitrary")),
        name="conv_ffn",
    )(x, g, init, wg, wu, wc, bc, wd)


def _block_diag(w):
    n, c, d = w.shape
    eye = jnp.eye(n, dtype=w.dtype)
    return (eye[:, None, :, None] * w[:, :, None, :]).reshape(n * c, n * d)


def _strict_tri(n):
    j = np.arange(n)[:, None]
    s = np.arange(n)[None, :]
    return jnp.asarray((j > s).astype(np.float32), dtype=BF16)


def kernel(x_prompt, x_sample, cache_k, cache_v, state_lru_h, state_lru_conv, state_ffn_conv, page_table,
           g_mix, g_ffn, w_in_even, g_q, g_k, sb_bias, w_conv_lru, b_conv_lru, w_rgate, b_rgate, w_igate,
           b_igate, lru_lambda, w_out_even, w_in_odd, g_v, w_spatial, b_spatial, w_out_odd,
           w_gate, w_up, w_ffn_conv, b_ffn_conv, w_down):
    bp, tp, _ = x_prompt.shape
    bs, ts, _ = x_sample.shape
    depth = g_mix.shape[0]
    n_pool = cache_k.shape[1]
    mp, ms = bp * tp, bs * ts
    assert ts == SUBLANES and tp % 512 == 0

    tm_p, tm_s = 512, ms
    tq = 256
    tt_p = 256
    tf = 1024
    n_group = 8

    yp = x_prompt.reshape(mp, D_MODEL)
    ys = x_sample.reshape(ms, D_MODEL)

    head_mean = jnp.asarray(np.kron(np.eye(H_A), np.full((HD_A, HD_A), 1.0 / HD_A)), dtype=BF16)
    tri = _strict_tri(tq)
    tri_s = _strict_tri(2 * PAGE)
    ck = jnp.transpose(cache_k, (0, 1, 3, 4, 2)).reshape(cache_k.shape[0], n_pool, W_A, PAGE)
    cv = jnp.transpose(cache_v, (0, 1, 3, 4, 2)).reshape(cache_v.shape[0], n_pool, W_A, PAGE)
    row_head = np.arange(H_A * SUBLANES)[:, None] // SUBLANES
    lane_head = np.arange(W_A)[None, :] // HD_A
    qbd_mask = jnp.asarray(row_head == lane_head)

    outs = {n: [] for n in ("kp", "vp", "ks", "vs", "hp", "hs", "cp", "cs", "chp", "chs", "fp", "fs")}
    last_chunk_start = ((tp - 1) // CHUNK) * CHUNK

    for l in range(depth):
        gm = g_mix[l].reshape(1, D_MODEL)
        if l % 2 == 0:
            e = l // 2
            w_in = w_in_even[e].astype(BF16)
            gq = (jnp.tile(g_q[e], H_A) * (HD_A ** -0.5 * LOG2E)).reshape(1, W_A)
            gk = jnp.tile(g_k[e], H_A).reshape(1, W_A)
            bias2 = sb_bias[e] * LOG2E
            wr = _block_diag(w_rgate[e]).astype(BF16)
            wi = _block_diag(w_igate[e]).astype(BF16)
            br = b_rgate[e].reshape(1, W_B)
            bi = b_igate[e].reshape(1, W_B)
            lam = lru_lambda[e].reshape(1, W_B)
            bc = b_conv_lru[e].reshape(1, W_B)
            w_out = w_out_even[e].astype(BF16)
            lru_w = (w_conv_lru[e], bc, wr, br, wi, bi, lam)

            q, k, v, kb, vb, xb, gb = _even_in(yp, gm, w_in, head_mean, gq, gk, tm_p)
            oa = _attn_prompt(q.reshape(bp, tp, W_A), kb.reshape(bp, tp, W_A), vb.reshape(bp, tp, W_A),
                              bias2, tri, tq)
            xb3 = xb.reshape(bp, tp, W_B)
            ob, hl = _lru(xb3, gb.reshape(bp, tp, W_B), jnp.zeros((bp, 1, W_B), F32),
                          jnp.zeros((bp, SUBLANES, W_B), F32), *lru_w, tt_p)
            yp = _even_out(yp, oa.reshape(mp, W_A), ob.reshape(mp, W_B), w_out, tm_p)
            outs["kp"].append(k.reshape(bp, tp, H_A, HD_A))
            outs["vp"].append(v.reshape(bp, tp, H_A, HD_A))
            outs["hp"].append(hl[:, SUBLANES - 1])
            outs["cp"].append(xb3[:, tp - (CONV_B - 1):])

            q, k, v, kb, vb, xb, gb = _even_in(ys, gm, w_in, head_mean, gq, gk, tm_s)
            qbd = jnp.where(qbd_mask, jnp.tile(q.reshape(bs, ts, W_A), (1, H_A, 1)), jnp.zeros((), BF16))
            brow = jnp.repeat(bias2, SUBLANES).reshape(H_A * SUBLANES, 1)
            pad = ((0, 0), (0, 0), (0, PAGE - ts))
            knew = jnp.pad(jnp.swapaxes(kb.reshape(bs, ts, W_A), 1, 2), pad)
            vnew = jnp.pad(jnp.swapaxes(vb.reshape(bs, ts, W_A), 1, 2), pad)
            oa = _attn_sample(page_table, qbd, brow, knew, vnew, tri_s, ck, cv, e, n_group)
            xb3 = xb.reshape(bs, ts, W_B)
            cbuf8 = jnp.pad(state_lru_conv[e], ((0, 0), (SUBLANES - (CONV_B - 1), 0), (0, 0)))
            ob, hl = _lru(xb3, gb.reshape(bs, ts, W_B), state_lru_h[e].reshape(bs, 1, W_B), cbuf8,
                          *lru_w, ts)
            ys = _even_out(ys, oa.reshape(ms, W_A), ob.reshape(ms, W_B), w_out, tm_s)
            outs["ks"].append(k.reshape(bs, ts, H_A, HD_A))
            outs["vs"].append(v.reshape(bs, ts, H_A, HD_A))
            outs["hs"].append(hl[:, SUBLANES - 1])
            outs["cs"].append(xb3[:, ts - (CONV_B - 1):])
        else:
            o = l // 2
            w_in = w_in_odd[o].astype(BF16)
            gv = g_v[o].reshape(1, W_C)
            w_out = w_out_odd[o].astype(BF16)
            w_tril = jnp.tril(w_spatial[o])
            wmix_p = w_tril.astype(BF16)
            bmix_p = jnp.repeat(b_spatial[o].T, CW_C, axis=1)
            eye = jnp.eye(bs, dtype=F32)
            wmix_s = (eye[None, :, None, :, None] * w_tril[:, None, :ts, None, :ts]
                      ).reshape(H_C, ms, ms).astype(BF16)
            bmix_s = jnp.tile(bmix_p[:ts], (bs, 1))

            yp, vn = _odd_mixer(yp, gm, w_in, gv, wmix_p, bmix_p, w_out, tm_p, CHUNK)
            outs["chp"].append(vn.reshape(bp, tp, W_C)[:, last_chunk_start:])
            ys, vn = _odd_mixer(ys, gm, w_in, gv, wmix_s, bmix_s, w_out, tm_s, ms)
            outs["chs"].append(vn.reshape(bs, ts, W_C))

        gf = g_ffn[l].reshape(1, D_MODEL)
        ffn_w = (w_gate[l].astype(BF16), w_up[l].astype(BF16), w_ffn_conv[l],
                 b_ffn_conv[l].reshape(1, D_FF), w_down[l].astype(BF16))
        yp, gt = _ffn(yp, gf, jnp.zeros((bp, SUBLANES, D_FF), F32), *ffn_w, tm_p, tf, tp)
        tiles = tp // tm_p
        outs["fp"].append(gt.reshape(bp, tiles, SUBLANES, D_FF)[:, tiles - 1, SUBLANES - (CONV_F - 1):])
        hist = jnp.pad(state_ffn_conv[l], ((0, 0), (0, SUBLANES - (CONV_F - 1)), (0, 0))).reshape(ms, D_FF)
        ys, gt = _ffn(ys, gf, hist, *ffn_w, tm_s, tf, ts)
        outs["fs"].append(gt.reshape(bs, ts, D_FF)[:, ts - (CONV_F - 1):])

    st = lambda n: jnp.stack(outs[n])
    return (yp.reshape(bp, tp, D_MODEL), ys.reshape(bs, ts, D_MODEL),
            st("kp"), st("vp"), st("ks"), st("vs"), st("hp"), st("hs"), st("cp"), st("cs"),
            st("chp"), st("chs"), st("fp"), st("fs"))
```

```python
import functools
import math

import numpy as np
import jax
import jax.numpy as jnp
from jax import lax
from jax.experimental import pallas as pl
from jax.experimental.pallas import tpu as pltpu

F32 = jnp.float32
BF16 = jnp.bfloat16

D_MODEL = 1024
H_A, HD_A = 8, 64
W_A = H_A * HD_A
H_B, BW_B = 8, 64
W_B = H_B * BW_B
CONV_B = 4
C_LRU = 8.0
H_C, CW_C = 8, 128
W_C = H_C * CW_C
CHUNK = 128
D_FF = 3 * D_MODEL
CONV_F = 3
PAGE = 128
EPS = 1e-6
D_IN_EVEN = 3 * W_A + 2 * W_B
LOG2E = math.log2(math.e)

LANES = 128
SUBLANES = 8
VMEM_LIMIT = 56 * 1024 * 1024


def _cparams(sem):
    return pltpu.CompilerParams(dimension_semantics=sem, vmem_limit_bytes=VMEM_LIMIT)


def _rms(x, g):
    ms = jnp.mean(x * x, axis=-1, keepdims=True)
    return x * lax.rsqrt(ms + EPS) * g


def _dot(a, b):
    return jnp.dot(a, b, preferred_element_type=F32)


def _dot_nt(a, b):
    return lax.dot_general(a, b, (((1,), (1,)), ((), ())), preferred_element_type=F32)


def _even_in_kernel(x_ref, g_ref, w_ref, pm_ref, gq_ref, gk_ref,
                    q_ref, k_ref, v_ref, kb_ref, vb_ref, xb_ref, gb_ref):
    xn = _rms(x_ref[...], g_ref[...]).astype(BF16)
    p = _dot(xn, w_ref[...])
    pm = pm_ref[...]

    def head_norm(t, g):
        sq = t * t
        hi = sq.astype(BF16)
        lo = (sq - hi.astype(F32)).astype(BF16)
        ms = _dot(hi, pm) + _dot(lo, pm)
        return t * lax.rsqrt(ms + EPS) * g

    q = head_norm(p[:, 0:W_A], gq_ref[...])
    k = head_norm(p[:, W_A:2 * W_A], gk_ref[...])
    v = p[:, 2 * W_A:3 * W_A]
    q_ref[...] = q.astype(BF16)
    k_ref[...] = k
    v_ref[...] = v
    kb_ref[...] = k.astype(BF16)
    vb_ref[...] = v.astype(BF16)
    xb_ref[...] = p[:, 3 * W_A:3 * W_A + W_B]
    gb_ref[...] = p[:, 3 * W_A + W_B:]


def _even_in(x, g, w, pm, gq, gk, tm):
    m = x.shape[0]
    row = lambda i: (i, 0)
    fixed = lambda i: (0, 0)
    f32o = jax.ShapeDtypeStruct((m, W_A), F32)
    bfo = jax.ShapeDtypeStruct((m, W_A), BF16)
    blk = pl.BlockSpec((tm, W_A), row)
    return pl.pallas_call(
        _even_in_kernel,
        grid=(m // tm,),
        in_specs=[pl.BlockSpec((tm, D_MODEL), row), pl.BlockSpec((1, D_MODEL), fixed),
                  pl.BlockSpec((D_MODEL, D_IN_EVEN), fixed), pl.BlockSpec((W_A, W_A), fixed),
                  pl.BlockSpec((1, W_A), fixed), pl.BlockSpec((1, W_A), fixed)],
        out_specs=[blk] * 7,
        out_shape=[bfo, f32o, f32o, bfo, bfo, f32o, f32o],
        compiler_params=_cparams(("parallel",)),
        name="even_in",
    )(x, g, w, pm, gq, gk)


def _softplus2(z):
    return jnp.maximum(z, jnp.log2(1.0 + jnp.exp2(jnp.minimum(z, 100.0))))


def _sb_weights(z, tri, carry, mask):
    sp = _softplus2(z)
    if mask is not None:
        sp = jnp.where(mask, sp, 0.0)
    c = _dot(sp.astype(BF16), tri)
    w = jnp.exp2(z - c - carry)
    if mask is not None:
        w = jnp.where(mask, w, 0.0)
    return w.astype(BF16), carry + c[:, 0:1]


def _attn_prompt_kernel(bias_ref, q_ref, k_ref, v_ref, tri_ref, o_ref, ka_ref, *, tk, nsub, unroll, chunk):
    pair = pl.program_id(1)
    i = pl.program_id(2)
    t = k_ref.shape[1]
    rows = 2 * nsub * tk

    @pl.when(i == 0)
    def _():
        ka_ref[:, 0:LANES] = k_ref[0]
        lane_t = lax.broadcasted_iota(jnp.int32, (t, LANES), 1)
        ka_ref[:, LANES:2 * LANES] = jnp.where(lane_t < 2, 1.0, 0.0).astype(BF16)

    q = q_ref[0]
    tri = tri_ref[...]
    lane = lax.broadcasted_iota(jnp.int32, (tk, LANES), 1)
    first = lane < HD_A
    zero = jnp.zeros((tk, LANES), BF16)
    ext = []
    for h in range(2):
        hi = bias_ref[0, 2 * pair + h]
        lo = bias_ref[1, 2 * pair + h]
        ext.append(jnp.where(lane == 0, hi, jnp.where(lane == 1, lo, 0.0)).astype(BF16))
    chains = []
    for s in range(nsub):
        qs = q[s * tk:(s + 1) * tk]
        chains.append(jnp.concatenate([jnp.where(first, qs, zero), ext[0]], axis=1))
        chains.append(jnp.concatenate([jnp.where(first, zero, qs), ext[1]], axis=1))
    qa = jnp.concatenate(chains, axis=0)

    def sweep(j_hi, nblk, qa_r, carry, acc, diag_rows):
        r = qa_r.shape[0]
        ch = min(chunk, r)
        for u in range(nblk):
            lo = pl.multiple_of((j_hi - u) * tk, tk)
            kj = ka_ref[pl.ds(lo, tk), :]
            vj = v_ref[0, pl.ds(lo, tk), :]
            cs, accs = [], []
            for c0 in range(0, r, ch):
                sl = slice(c0, c0 + ch)
                z = _dot_nt(qa_r[sl], kj)
                mask = None
                if diag_rows and u == 0 and c0 < diag_rows:
                    rr = lax.broadcasted_iota(jnp.int32, (ch, tk), 0) + c0
                    cc = lax.broadcasted_iota(jnp.int32, (ch, tk), 1)
                    mask = (rr >= diag_rows) | (cc < (rr % tk))
                w, c2 = _sb_weights(z, tri, carry[sl], mask)
                cs.append(c2)
                accs.append(acc[sl] + _dot(w, vj))
            carry = cs[0] if len(cs) == 1 else jnp.concatenate(cs, axis=0)
            acc = accs[0] if len(accs) == 1 else jnp.concatenate(accs, axis=0)
        return carry, acc

    carry = jnp.zeros((rows, 1), F32)
    acc = jnp.zeros((rows, LANES), F32)
    for m in reversed(range(nsub)):
        r0 = m * 2 * tk
        c_m, a_m = sweep(nsub * i + m, 1, qa[r0:], carry[r0:], acc[r0:], 2 * tk)
        carry = jnp.concatenate([carry[:r0], c_m], axis=0) if r0 else c_m
        acc = jnp.concatenate([acc[:r0], a_m], axis=0) if r0 else a_m

    nloop = nsub * i
    carry, acc = lax.fori_loop(
        0, nloop // unroll,
        lambda n, st: sweep(nloop - 1 - unroll * n, unroll, qa, st[0], st[1], 0), (carry, acc))
    for s in range(nsub):
        a0 = acc[(2 * s) * tk:(2 * s + 1) * tk]
        a1 = acc[(2 * s + 1) * tk:(2 * s + 2) * tk]
        o_ref[0, s * tk:(s + 1) * tk, :] = jnp.where(first, a0, a1).astype(o_ref.dtype)


def _attn_prompt(q, k, v, bias_hl, tri, tk, nsub, unroll, chunk):
    b, t, _ = q.shape
    npair = W_A // LANES
    tq = tk * nsub
    assert t % tq == 0 and nsub % unroll == 0
    return pl.pallas_call(
        functools.partial(_attn_prompt_kernel, tk=tk, nsub=nsub, unroll=unroll, chunk=chunk),
        grid=(b, npair, t // tq),
        in_specs=[pl.BlockSpec(memory_space=pltpu.SMEM),
                  pl.BlockSpec((1, tq, LANES), lambda bi, p, i: (bi, i, p)),
                  pl.BlockSpec((1, t, LANES), lambda bi, p, i: (bi, 0, p)),
                  pl.BlockSpec((1, t, LANES), lambda bi, p, i: (bi, 0, p)),
                  pl.BlockSpec((tk, tk), lambda bi, p, i: (0, 0))],
        out_specs=pl.BlockSpec((1, tq, LANES), lambda bi, p, i: (bi, i, p)),
        out_shape=jax.ShapeDtypeStruct((b, t, W_A), BF16),
        scratch_shapes=[pltpu.VMEM((t, 2 * LANES), BF16)],
        compiler_params=_cparams(("parallel", "parallel", "arbitrary")),
        name="attn_prompt",
    )(bias_hl, q, k, v, tri)


def _attn_sample_kernel(pt_ref, qbd_ref, brow_ref, kn_ref, vn_ref, tri_ref, *rest, n_group):
    del pt_ref
    pages = rest[:2 * n_group]
    o_ref, acc_ref, car_ref = rest[2 * n_group:]
    p = pl.program_id(1)
    qbd = qbd_ref[0]
    brow = brow_ref[...]
    tri = tri_ref[...]
    rows = H_A * SUBLANES
    blk = 2 * PAGE
    nblk = n_group * PAGE // blk

    @pl.when(p == 0)
    def _():
        z = _dot(qbd, kn_ref[0]) + brow
        rr = lax.broadcasted_iota(jnp.int32, (rows, PAGE), 0) % SUBLANES
        cc = lax.broadcasted_iota(jnp.int32, (rows, PAGE), 1)
        w, carry = _sb_weights(z, tri[0:PAGE, 0:PAGE], jnp.zeros((rows, 1), F32), cc < rr)
        acc_ref[...] = _dot_nt(w, vn_ref[0])
        car_ref[...] = carry

    kall = jnp.concatenate([pages[g][0, 0] for g in range(n_group)], axis=1).astype(BF16)
    vall = jnp.concatenate([pages[n_group + g][0, 0] for g in range(n_group)], axis=1).astype(BF16)
    z = _dot(qbd, kall) + brow
    spb = _softplus2(z).astype(BF16)
    cs = [_dot(spb[:, b * blk:(b + 1) * blk], tri) for b in range(nblk)]
    run = car_ref[...]
    args = [None] * nblk
    for b in reversed(range(nblk)):
        args[b] = z[:, b * blk:(b + 1) * blk] - cs[b] - run
        run = run + cs[b][:, 0:1]
    w = jnp.exp2(jnp.concatenate(args, axis=1)).astype(BF16)
    car_ref[...] = run
    acc_ref[...] += _dot_nt(w, vall)

    @pl.when(p == pl.num_programs(1) - 1)
    def _():
        acc = acc_ref[...]
        lane_head = lax.broadcasted_iota(jnp.int32, (SUBLANES, W_A), 1) // HD_A
        out = jnp.zeros((SUBLANES, W_A), F32)
        for h in range(H_A):
            out = jnp.where(lane_head == h, acc[h * SUBLANES:(h + 1) * SUBLANES, :], out)
        o_ref[0] = out.astype(o_ref.dtype)


def _attn_sample(page_table, qbd, brow, knew, vnew, tri, cache_k, cache_v, e, n_group):
    bs, n_pages = page_table.shape
    steps = n_pages // n_group
    rows = H_A * SUBLANES

    def page_spec(i):
        return pl.BlockSpec(
            (1, 1, W_A, PAGE),
            lambda b, p, pt, i=i: (e, pt[b, (steps - 1 - p) * n_group + i], 0, 0))

    grid_spec = pltpu.PrefetchScalarGridSpec(
        num_scalar_prefetch=1,
        grid=(bs, steps),
        in_specs=[pl.BlockSpec((1, rows, W_A), lambda b, p, pt: (b, 0, 0)),
                  pl.BlockSpec((rows, 1), lambda b, p, pt: (0, 0)),
                  pl.BlockSpec((1, W_A, PAGE), lambda b, p, pt: (b, 0, 0)),
                  pl.BlockSpec((1, W_A, PAGE), lambda b, p, pt: (b, 0, 0)),
                  pl.BlockSpec((2 * PAGE, 2 * PAGE), lambda b, p, pt: (0, 0))]
                 + [page_spec(i) for i in range(n_group)] * 2,
        out_specs=pl.BlockSpec((1, SUBLANES, W_A), lambda b, p, pt: (b, 0, 0)),
        scratch_shapes=[pltpu.VMEM((rows, W_A), F32), pltpu.VMEM((rows, 1), F32)],
    )
    return pl.pallas_call(
        functools.partial(_attn_sample_kernel, n_group=n_group),
        grid_spec=grid_spec,
        out_shape=jax.ShapeDtypeStruct((bs, SUBLANES, W_A), BF16),
        compiler_params=_cparams(("parallel", "arbitrary")),
        name="attn_sample",
    )(page_table, qbd, brow, knew, vnew, tri, *([cache_k] * n_group), *([cache_v] * n_group))


def _lru_kernel(xb_ref, gb_ref, h0_ref, cb_ref, wc_ref, bc_ref, wr_ref, br_ref, wi_ref, bi_ref, lam_ref,
                ob_ref, hl_ref, xs_ref, hc_ref, *, tt):
    t = pl.program_id(1)

    @pl.when(t == 0)
    def _():
        xs_ref[0:SUBLANES, :] = cb_ref[0]
        hc_ref[...] = h0_ref[0]

    x = xb_ref[0]
    xs_ref[SUBLANES:SUBLANES + tt, :] = x
    wc = wc_ref[...]
    xc = (bc_ref[...] + wc[3:4] * x + wc[2:3] * xs_ref[SUBLANES - 1:SUBLANES - 1 + tt, :]
          + wc[1:2] * xs_ref[SUBLANES - 2:SUBLANES - 2 + tt, :]
          + wc[0:1] * xs_ref[SUBLANES - 3:SUBLANES - 3 + tt, :])
    xs_ref[0:SUBLANES, :] = x[tt - SUBLANES:tt]

    xcb = xc.astype(BF16)
    r = jax.nn.sigmoid(_dot(xcb, wr_ref[...]) + br_ref[...])
    ig = jax.nn.sigmoid(_dot(xcb, wi_ref[...]) + bi_ref[...])
    nl = -lam_ref[...]
    sp_lam = jnp.maximum(nl, 0.0) + jnp.log1p(jnp.exp(-jnp.abs(nl)))
    log_a = -C_LRU * r * sp_lam
    a = jnp.exp(log_a)
    u = jnp.sqrt((1.0 + a * a) * jnp.tanh(-log_a)) * (ig * xc)
    row = lax.broadcasted_iota(jnp.int32, (tt, W_B), 0)
    u = u + jnp.where(row == 0, a * hc_ref[...], 0.0)

    d = 1
    while d < tt:
        keep = row >= d
        a_sh = jnp.where(keep, pltpu.roll(a, d, 0), 1.0)
        u_sh = jnp.where(keep, pltpu.roll(u, d, 0), 0.0)
        u = a * u_sh + u
        a = a * a_sh
        d *= 2
    h = u
    hc_ref[...] = h[tt - 1:tt]
    hl_ref[0] = h[tt - SUBLANES:tt]
    ob_ref[0] = (h * jax.nn.gelu(gb_ref[0])).astype(ob_ref.dtype)


def _lru(xb, gb, h0, cbuf8, wc, bc, wr, br, wi, bi, lam, tt):
    b, t, _ = xb.shape
    seq = lambda bi, ti: (bi, ti, 0)
    per_b = lambda bi, ti: (bi, 0, 0)
    fixed = lambda bi, ti: (0, 0)
    return pl.pallas_call(
        functools.partial(_lru_kernel, tt=tt),
        grid=(b, t // tt),
        in_specs=[pl.BlockSpec((1, tt, W_B), seq), pl.BlockSpec((1, tt, W_B), seq),
                  pl.BlockSpec((1, 1, W_B), per_b), pl.BlockSpec((1, SUBLANES, W_B), per_b),
                  pl.BlockSpec((CONV_B, W_B), fixed), pl.BlockSpec((1, W_B), fixed),
                  pl.BlockSpec((W_B, W_B), fixed), pl.BlockSpec((1, W_B), fixed),
                  pl.BlockSpec((W_B, W_B), fixed), pl.BlockSpec((1, W_B), fixed),
                  pl.BlockSpec((1, W_B), fixed)],
        out_specs=[pl.BlockSpec((1, tt, W_B), seq), pl.BlockSpec((1, SUBLANES, W_B), per_b)],
        out_shape=[jax.ShapeDtypeStruct((b, t, W_B), BF16),
                   jax.ShapeDtypeStruct((b, SUBLANES, W_B), F32)],
        scratch_shapes=[pltpu.VMEM((tt + SUBLANES, W_B), F32), pltpu.VMEM((1, W_B), F32)],
        compiler_params=_cparams(("parallel", "arbitrary")),
        name="rglru",
    )(xb, gb, h0, cbuf8, wc, bc, wr, br, wi, bi, lam)


def _even_out_kernel(x_ref, oa_ref, ob_ref, w_ref, y_ref):
    y_ref[...] = (x_ref[...] + _dot(oa_ref[...], w_ref[0:W_A, :])
                  + _dot(ob_ref[...], w_ref[W_A:W_A + W_B, :]))


def _even_out(x, oa, ob, w, tm):
    m = x.shape[0]
    row = lambda i: (i, 0)
    return pl.pallas_call(
        _even_out_kernel,
        grid=(m // tm,),
        in_specs=[pl.BlockSpec((tm, D_MODEL), row), pl.BlockSpec((tm, W_A), row),
                  pl.BlockSpec((tm, W_B), row), pl.BlockSpec((W_A + W_B, D_MODEL), lambda i: (0, 0))],
        out_specs=pl.BlockSpec((tm, D_MODEL), row),
        out_shape=jax.ShapeDtypeStruct((m, D_MODEL), F32),
        compiler_params=_cparams(("parallel",)),
        name="even_out",
    )(x, oa, ob, w)


def _odd_kernel(x_ref, g_ref, win_ref, gv_ref, wmix_ref, bmix_ref, wout_ref, y_ref, vn_ref, um_ref,
                *, tm, rblk):
    x = x_ref[...]
    xn = _rms(x, g_ref[...]).astype(BF16)
    z = jax.nn.gelu(_dot(xn, win_ref[...]))
    for h in range(H_C):
        lo, hi = h * CW_C, (h + 1) * CW_C
        vh = z[:, W_C + lo:W_C + hi]
        ms = jnp.mean(vh * vh, axis=-1, keepdims=True)
        vnh = vh * lax.rsqrt(ms + EPS) * gv_ref[:, lo:hi]
        vn_ref[:, lo:hi] = vnh
        vnb = vnh.astype(BF16)
        wm = wmix_ref[h]
        for c in range(tm // rblk):
            r0, r1 = c * rblk, (c + 1) * rblk
            mixed = _dot(wm, vnb[r0:r1]) + bmix_ref[:, lo:hi]
            um_ref[r0:r1, lo:hi] = (z[r0:r1, lo:hi] * mixed).astype(BF16)
    y_ref[...] = x + _dot(um_ref[...], wout_ref[...])


def _odd_mixer(x, g, win, gv, wmix, bmix, wout, tm, rblk):
    m = x.shape[0]
    row = lambda i: (i, 0)
    fixed = lambda i: (0, 0)
    return pl.pallas_call(
        functools.partial(_odd_kernel, tm=tm, rblk=rblk),
        grid=(m // tm,),
        in_specs=[pl.BlockSpec((tm, D_MODEL), row), pl.BlockSpec((1, D_MODEL), fixed),
                  pl.BlockSpec((D_MODEL, 2 * W_C), fixed), pl.BlockSpec((1, W_C), fixed),
                  pl.BlockSpec((H_C, rblk, rblk), lambda i: (0, 0, 0)),
                  pl.BlockSpec((rblk, W_C), fixed), pl.BlockSpec((W_C, D_MODEL), fixed)],
        out_specs=[pl.BlockSpec((tm, D_MODEL), row), pl.BlockSpec((tm, W_C), row)],
        out_shape=[jax.ShapeDtypeStruct((m, D_MODEL), F32), jax.ShapeDtypeStruct((m, W_C), F32)],
        scratch_shapes=[pltpu.VMEM((tm, W_C), BF16)],
        compiler_params=_cparams(("parallel",)),
        name="odd_mixer",
    )(x, g, win, gv, wmix, bmix, wout)


def _ffn_kernel(x_ref, g_ref, init_ref, wg_ref, wu_ref, wc_ref, bc_ref, wd_ref,
                y_ref, gt_ref, xn_ref, car_ref, *, tm, tiles_per_seq, short_seq):
    m = pl.program_id(0)
    j = pl.program_id(1)

    @pl.when(j == 0)
    def _():
        x = x_ref[...]
        xn_ref[...] = _rms(x, g_ref[...]).astype(BF16)
        y_ref[...] = x

    if not short_seq:
        @pl.when(m % tiles_per_seq == 0)
        def _():
            car_ref[j] = init_ref[0]

    xn = xn_ref[...]
    g = _dot(xn, wg_ref[...])
    u = _dot(xn, wu_ref[...])
    g1 = pltpu.roll(g, 1, 0)
    g2 = pltpu.roll(g, 2, 0)
    if short_seq:
        hist = init_ref[...]
        r8 = lax.broadcasted_iota(jnp.int32, g.shape, 0) % SUBLANES
        p1 = jnp.where(r8 == 0, pltpu.roll(hist, tm - 1, 0), g1)
        p2 = jnp.where(r8 < 2, hist, g2)
        gt_ref[...] = g
    else:
        prev = car_ref[j]
        r8 = lax.broadcasted_iota(jnp.int32, prev.shape, 0)
        p1 = jnp.concatenate([jnp.where(r8 < 1, pltpu.roll(prev, 1, 0), g1[0:SUBLANES]), g1[SUBLANES:]], axis=0)
        p2 = jnp.concatenate([jnp.where(r8 < 2, pltpu.roll(prev, 2, 0), g2[0:SUBLANES]), g2[SUBLANES:]], axis=0)
        tail = g[tm - SUBLANES:tm]
        car_ref[j] = tail
        gt_ref[0] = tail
    wc = wc_ref[...]
    gc = bc_ref[...] + wc[2:3] * g + wc[1:2] * p1 + wc[0:1] * p2
    hid = (jax.nn.gelu(gc) * u).astype(BF16)
    y_ref[...] += _dot(hid, wd_ref[...])


def _ffn(x, g, init, wg, wu, wc, bc, wd, tm, tf, seq_len):
    m = x.shape[0]
    nff = D_FF // tf
    short_seq = seq_len < tm
    if short_seq:
        assert seq_len == SUBLANES and m == tm
        tiles_per_seq = 1
        init_spec = pl.BlockSpec((tm, tf), lambda i, j: (i, j))
        gt_spec = pl.BlockSpec((tm, tf), lambda i, j: (i, j))
        gt_shape = jax.ShapeDtypeStruct((m, D_FF), F32)
    else:
        assert seq_len % tm == 0
        tiles_per_seq = seq_len // tm
        init_spec = pl.BlockSpec((1, SUBLANES, tf), lambda i, j: (i // tiles_per_seq, 0, j))
        gt_spec = pl.BlockSpec((1, SUBLANES, tf), lambda i, j: (i, 0, j))
        gt_shape = jax.ShapeDtypeStruct((m // tm, SUBLANES, D_FF), F32)
    return pl.pallas_call(
        functools.partial(_ffn_kernel, tm=tm, tiles_per_seq=tiles_per_seq, short_seq=short_seq),
        grid=(m // tm, nff),
        in_specs=[pl.BlockSpec((tm, D_MODEL), lambda i, j: (i, 0)),
                  pl.BlockSpec((1, D_MODEL), lambda i, j: (0, 0)),
                  init_spec,
                  pl.BlockSpec((D_MODEL, tf), lambda i, j: (0, j)),
                  pl.BlockSpec((D_MODEL, tf), lambda i, j: (0, j)),
                  pl.BlockSpec((CONV_F, tf), lambda i, j: (0, j)),
                  pl.BlockSpec((1, tf), lambda i, j: (0, j)),
                  pl.BlockSpec((tf, D_MODEL), lambda i, j: (j, 0))],
        out_specs=[pl.BlockSpec((tm, D_MODEL), lambda i, j: (i, 0)), gt_spec],
        out_shape=[jax.ShapeDtypeStruct((m, D_MODEL), F32), gt_shape],
        scratch_shapes=[pltpu.VMEM((tm, D_MODEL), BF16), pltpu.VMEM((nff, SUBLANES, tf), F32)],
        compiler_params=_cparams(("arbitrary", "arbitrary")),
        name="conv_ffn",
    )(x, g, init, wg, wu, wc, bc, wd)


def _block_diag(w):
    n, c, d = w.shape
    eye = jnp.eye(n, dtype=w.dtype)
    return (eye[:, None, :, None] * w[:, :, None, :]).reshape(n * c, n * d)


def _suffix_tri(n):
    j = np.arange(n)[:, None]
    s = np.arange(n)[None, :]
    return jnp.asarray((j >= s).astype(np.float32), dtype=BF16)


def kernel(x_prompt, x_sample, cache_k, cache_v, state_lru_h, state_lru_conv, state_ffn_conv, page_table,
           g_mix, g_ffn, w_in_even, g_q, g_k, sb_bias, w_conv_lru, b_conv_lru, w_rgate, b_rgate, w_igate,
           b_igate, lru_lambda, w_out_even, w_in_odd, g_v, w_spatial, b_spatial, w_out_odd,
           w_gate, w_up, w_ffn_conv, b_ffn_conv, w_down):
    bp, tp, _ = x_prompt.shape
    bs, ts, _ = x_sample.shape
    depth = g_mix.shape[0]
    n_pool = cache_k.shape[1]
    mp, ms = bp * tp, bs * ts
    assert ts == SUBLANES and tp % 512 == 0

    tm_p, tm_s = 512, ms
    tm_f = 1024
    tk, nsub, unroll, chunk = 256, 4, 4, 1024
    tt_p = 256
    tf = 1024
    n_group = 8

    yp = x_prompt.reshape(mp, D_MODEL)
    ys = x_sample.reshape(ms, D_MODEL)

    head_mean = jnp.asarray(np.kron(np.eye(H_A), np.full((HD_A, HD_A), 1.0 / HD_A)), dtype=BF16)
    tri = _suffix_tri(tk)
    tri_s = _suffix_tri(2 * PAGE)
    ck = jnp.transpose(cache_k, (0, 1, 3, 4, 2)).reshape(cache_k.shape[0], n_pool, W_A, PAGE)
    cv = jnp.transpose(cache_v, (0, 1, 3, 4, 2)).reshape(cache_v.shape[0], n_pool, W_A, PAGE)
    row_head = np.arange(H_A * SUBLANES)[:, None] // SUBLANES
    lane_head = np.arange(W_A)[None, :] // HD_A
    qbd_mask = jnp.asarray(row_head == lane_head)

    outs = {n: [] for n in ("kp", "vp", "ks", "vs", "hp", "hs", "cp", "cs", "chp", "chs", "fp", "fs")}
    last_chunk_start = ((tp - 1) // CHUNK) * CHUNK

    for l in range(depth):
        gm = g_mix[l].reshape(1, D_MODEL)
        if l % 2 == 0:
            e = l // 2
            w_in = w_in_even[e].astype(BF16)
            gq = (jnp.tile(g_q[e], H_A) * (HD_A ** -0.5 * LOG2E)).reshape(1, W_A)
            gk = jnp.tile(g_k[e], H_A).reshape(1, W_A)
            bias2 = sb_bias[e] * LOG2E
            bias_hi = bias2.astype(BF16).astype(F32)
            bias_hl = jnp.stack([bias_hi, (bias2 - bias_hi).astype(BF16).astype(F32)])
            wr = _block_diag(w_rgate[e]).astype(BF16)
            wi = _block_diag(w_igate[e]).astype(BF16)
            br = b_rgate[e].reshape(1, W_B)
            bi = b_igate[e].reshape(1, W_B)
            lam = lru_lambda[e].reshape(1, W_B)
            bc = b_conv_lru[e].reshape(1, W_B)
            w_out = w_out_even[e].astype(BF16)
            lru_w = (w_conv_lru[e], bc, wr, br, wi, bi, lam)

            q, k, v, kb, vb, xb, gb = _even_in(yp, gm, w_in, head_mean, gq, gk, tm_p)
            oa = _attn_prompt(q.reshape(bp, tp, W_A), kb.reshape(bp, tp, W_A), vb.reshape(bp, tp, W_A),
                              bias_hl, tri, tk, nsub, unroll, chunk)
            xb3 = xb.reshape(bp, tp, W_B)
            ob, hl = _lru(xb3, gb.reshape(bp, tp, W_B), jnp.zeros((bp, 1, W_B), F32),
                          jnp.zeros((bp, SUBLANES, W_B), F32), *lru_w, tt_p)
            yp = _even_out(yp, oa.reshape(mp, W_A), ob.reshape(mp, W_B), w_out, tm_p)
            outs["kp"].append(k.reshape(bp, tp, H_A, HD_A))
            outs["vp"].append(v.reshape(bp, tp, H_A, HD_A))
            outs["hp"].append(hl[:, SUBLANES - 1])
            outs["cp"].append(xb3[:, tp - (CONV_B - 1):])

            q, k, v, kb, vb, xb, gb = _even_in(ys, gm, w_in, head_mean, gq, gk, tm_s)
            qbd = jnp.where(qbd_mask, jnp.tile(q.reshape(bs, ts, W_A), (1, H_A, 1)), jnp.zeros((), BF16))
            brow = jnp.repeat(bias2, SUBLANES).reshape(H_A * SUBLANES, 1)
            pad = ((0, 0), (0, 0), (0, PAGE - ts))
            knew = jnp.pad(jnp.swapaxes(kb.reshape(bs, ts, W_A), 1, 2), pad)
            vnew = jnp.pad(jnp.swapaxes(vb.reshape(bs, ts, W_A), 1, 2), pad)
            oa = _attn_sample(page_table, qbd, brow, knew, vnew, tri_s, ck, cv, e, n_group)
            xb3 = xb.reshape(bs, ts, W_B)
            cbuf8 = jnp.pad(state_lru_conv[e], ((0, 0), (SUBLANES - (CONV_B - 1), 0), (0, 0)))
            ob, hl = _lru(xb3, gb.reshape(bs, ts, W_B), state_lru_h[e].reshape(bs, 1, W_B), cbuf8,
                          *lru_w, ts)
            ys = _even_out(ys, oa.reshape(ms, W_A), ob.reshape(ms, W_B), w_out, tm_s)
            outs["ks"].append(k.reshape(bs, ts, H_A, HD_A))
            outs["vs"].append(v.reshape(bs, ts, H_A, HD_A))
            outs["hs"].append(hl[:, SUBLANES - 1])
            outs["cs"].append(xb3[:, ts - (CONV_B - 1):])
        else:
            o = l // 2
            w_in = w_in_odd[o].astype(BF16)
            gv = g_v[o].reshape(1, W_C)
            w_out = w_out_odd[o].astype(BF16)
            w_tril = jnp.tril(w_spatial[o])
            wmix_p = w_tril.astype(BF16)
            bmix_p = jnp.repeat(b_spatial[o].T, CW_C, axis=1)
            eye = jnp.eye(bs, dtype=F32)
            wmix_s = (eye[None, :, None, :, None] * w_tril[:, None, :ts, None, :ts]
                      ).reshape(H_C, ms, ms).astype(BF16)
            bmix_s = jnp.tile(bmix_p[:ts], (bs, 1))

            yp, vn = _odd_mixer(yp, gm, w_in, gv, wmix_p, bmix_p, w_out, tm_p, CHUNK)
            outs["chp"].append(vn.reshape(bp, tp, W_C)[:, last_chunk_start:])
            ys, vn = _odd_mixer(ys, gm, w_in, gv, wmix_s, bmix_s, w_out, tm_s, ms)
            outs["chs"].append(vn.reshape(bs, ts, W_C))

        gf = g_ffn[l].reshape(1, D_MODEL)
        ffn_w = (w_gate[l].astype(BF16), w_up[l].astype(BF16), w_ffn_conv[l],
                 b_ffn_conv[l].reshape(1, D_FF), w_down[l].astype(BF16))
        yp, gt = _ffn(yp, gf, jnp.zeros((bp, SUBLANES, D_FF), F32), *ffn_w, tm_f, tf, tp)
        tiles = tp // tm_f
        outs["fp"].append(gt.reshape(bp, tiles, SUBLANES, D_FF)[:, tiles - 1, SUBLANES - (CONV_F - 1):])
        hist = jnp.pad(state_ffn_conv[l], ((0, 0), (0, SUBLANES - (CONV_F - 1)), (0, 0))).reshape(ms, D_FF)
        ys, gt = _ffn(ys, gf, hist, *ffn_w, tm_s, tf, ts)
        outs["fs"].append(gt.reshape(bs, ts, D_FF)[:, ts - (CONV_F - 1):])

    st = lambda n: jnp.stack(outs[n])
    return (yp.reshape(bp, tp, D_MODEL), ys.reshape(bs, ts, D_MODEL),
            st("kp"), st("vp"), st("ks"), st("vs"), st("hp"), st("hs"), st("cp"), st("cs"),
            st("chp"), st("chs"), st("fp"), st("fs"))
```

```python
import functools
import math

import numpy as np
import jax
import jax.numpy as jnp
from jax import lax
from jax.experimental import pallas as pl
from jax.experimental.pallas import tpu as pltpu

F32 = jnp.float32
BF16 = jnp.bfloat16

D_MODEL = 1024
H_A, HD_A = 8, 64
W_A = H_A * HD_A
H_B, BW_B = 8, 64
W_B = H_B * BW_B
CONV_B = 4
C_LRU = 8.0
H_C, CW_C = 8, 128
W_C = H_C * CW_C
CHUNK = 128
D_FF = 3 * D_MODEL
CONV_F = 3
PAGE = 128
EPS = 1e-6
D_IN_EVEN = 3 * W_A + 2 * W_B
LOG2E = math.log2(math.e)

LANES = 128
SUBLANES = 8
VMEM_LIMIT = 56 * 1024 * 1024


def _cparams(sem):
    return pltpu.CompilerParams(dimension_semantics=sem, vmem_limit_bytes=VMEM_LIMIT)


def _rms(x, g):
    ms = jnp.mean(x * x, axis=-1, keepdims=True)
    return x * lax.rsqrt(ms + EPS) * g


def _dot(a, b):
    return jnp.dot(a, b, preferred_element_type=F32)


def _dot_nt(a, b):
    return lax.dot_general(a, b, (((1,), (1,)), ((), ())), preferred_element_type=F32)


def _even_in_kernel(x_ref, g_ref, w_ref, pm_ref, gq_ref, gk_ref,
                    q_ref, k_ref, v_ref, kb_ref, vb_ref, xb_ref, gb_ref, *, kv_transposed):
    xn = _rms(x_ref[...], g_ref[...]).astype(BF16)
    p = _dot(xn, w_ref[...])
    pm = pm_ref[...]

    def head_norm(t, g):
        sq = t * t
        hi = sq.astype(BF16)
        lo = (sq - hi.astype(F32)).astype(BF16)
        ms = _dot(hi, pm) + _dot(lo, pm)
        return t * lax.rsqrt(ms + EPS) * g

    q = head_norm(p[:, 0:W_A], gq_ref[...])
    k = head_norm(p[:, W_A:2 * W_A], gk_ref[...])
    v = p[:, 2 * W_A:3 * W_A]
    q_ref[...] = q.astype(BF16)
    if kv_transposed:
        k_ref[0] = k.T
        v_ref[0] = v.T
    else:
        k_ref[...] = k
        v_ref[...] = v
    kb_ref[...] = k.astype(BF16)
    vb_ref[...] = v.astype(BF16)
    xb_ref[...] = p[:, 3 * W_A:3 * W_A + W_B]
    gb_ref[...] = p[:, 3 * W_A + W_B:]


def _even_in(x, g, w, pm, gq, gk, tm, seq_len):
    m = x.shape[0]
    row = lambda i: (i, 0)
    fixed = lambda i: (0, 0)
    f32o = jax.ShapeDtypeStruct((m, W_A), F32)
    bfo = jax.ShapeDtypeStruct((m, W_A), BF16)
    blk = pl.BlockSpec((tm, W_A), row)
    kv_shape, kv_blk = f32o, blk
    if seq_len is not None:
        tps = seq_len // tm
        kv_shape = jax.ShapeDtypeStruct((m // seq_len, W_A, seq_len), F32)
        kv_blk = pl.BlockSpec((1, W_A, tm), lambda i: (i // tps, 0, i % tps))
    return pl.pallas_call(
        functools.partial(_even_in_kernel, kv_transposed=seq_len is not None),
        grid=(m // tm,),
        in_specs=[pl.BlockSpec((tm, D_MODEL), row), pl.BlockSpec((1, D_MODEL), fixed),
                  pl.BlockSpec((D_MODEL, D_IN_EVEN), fixed), pl.BlockSpec((W_A, W_A), fixed),
                  pl.BlockSpec((1, W_A), fixed), pl.BlockSpec((1, W_A), fixed)],
        out_specs=[blk, kv_blk, kv_blk, blk, blk, blk, blk],
        out_shape=[bfo, kv_shape, kv_shape, bfo, bfo, f32o, f32o],
        compiler_params=_cparams(("parallel",)),
        name="even_in",
    )(x, g, w, pm, gq, gk)


def _softplus2(z):
    return jnp.maximum(z, jnp.log2(1.0 + jnp.exp2(jnp.minimum(z, 100.0))))


def _sb_weights(z, tri, carry, mask):
    sp = _softplus2(z)
    if mask is not None:
        sp = jnp.where(mask, sp, 0.0)
    c = _dot(sp.astype(BF16), tri)
    w = jnp.exp2(z - c - carry)
    if mask is not None:
        w = jnp.where(mask, w, 0.0)
    return w.astype(BF16), carry + c[:, 0:1]


def _attn_prompt_kernel(bias_ref, q_ref, k_ref, v_ref, tri_ref, o_ref, ka_ref, *, tk, nsub, unroll, chunk):
    pair = pl.program_id(1)
    i = pl.program_id(2)
    t = k_ref.shape[1]
    rows = 2 * nsub * tk

    @pl.when(i == 0)
    def _():
        ka_ref[:, 0:LANES] = k_ref[0]
        lane_t = lax.broadcasted_iota(jnp.int32, (t, LANES), 1)
        ka_ref[:, LANES:2 * LANES] = jnp.where(lane_t < 2, 1.0, 0.0).astype(BF16)

    q = q_ref[0]
    tri = tri_ref[...]
    lane = lax.broadcasted_iota(jnp.int32, (tk, LANES), 1)
    first = lane < HD_A
    zero = jnp.zeros((tk, LANES), BF16)
    ext = []
    for h in range(2):
        hi = bias_ref[0, 2 * pair + h]
        lo = bias_ref[1, 2 * pair + h]
        ext.append(jnp.where(lane == 0, hi, jnp.where(lane == 1, lo, 0.0)).astype(BF16))
    chains = []
    for s in range(nsub):
        qs = q[s * tk:(s + 1) * tk]
        chains.append(jnp.concatenate([jnp.where(first, qs, zero), ext[0]], axis=1))
        chains.append(jnp.concatenate([jnp.where(first, zero, qs), ext[1]], axis=1))
    qa = jnp.concatenate(chains, axis=0)

    def sweep(j_hi, nblk, qa_r, carry, acc, diag_rows):
        r = qa_r.shape[0]
        ch = min(chunk, r)
        for u in range(nblk):
            lo = pl.multiple_of((j_hi - u) * tk, tk)
            kj = ka_ref[pl.ds(lo, tk), :]
            vj = v_ref[0, pl.ds(lo, tk), :]
            cs, accs = [], []
            for c0 in range(0, r, ch):
                sl = slice(c0, c0 + ch)
                z = _dot_nt(qa_r[sl], kj)
                mask = None
                if diag_rows and u == 0 and c0 < diag_rows:
                    rr = lax.broadcasted_iota(jnp.int32, (ch, tk), 0) + c0
                    cc = lax.broadcasted_iota(jnp.int32, (ch, tk), 1)
                    mask = (rr >= diag_rows) | (cc < (rr % tk))
                w, c2 = _sb_weights(z, tri, carry[sl], mask)
                cs.append(c2)
                accs.append(acc[sl] + _dot(w, vj))
            carry = cs[0] if len(cs) == 1 else jnp.concatenate(cs, axis=0)
            acc = accs[0] if len(accs) == 1 else jnp.concatenate(accs, axis=0)
        return carry, acc

    carry = jnp.zeros((rows, 1), F32)
    acc = jnp.zeros((rows, LANES), F32)
    for m in reversed(range(nsub)):
        r0 = m * 2 * tk
        c_m, a_m = sweep(nsub * i + m, 1, qa[r0:], carry[r0:], acc[r0:], 2 * tk)
        carry = jnp.concatenate([carry[:r0], c_m], axis=0) if r0 else c_m
        acc = jnp.concatenate([acc[:r0], a_m], axis=0) if r0 else a_m

    nloop = nsub * i
    carry, acc = lax.fori_loop(
        0, nloop // unroll,
        lambda n, st: sweep(nloop - 1 - unroll * n, unroll, qa, st[0], st[1], 0), (carry, acc))
    for s in range(nsub):
        a0 = acc[(2 * s) * tk:(2 * s + 1) * tk]
        a1 = acc[(2 * s + 1) * tk:(2 * s + 2) * tk]
        o_ref[0, s * tk:(s + 1) * tk, :] = jnp.where(first, a0, a1).astype(o_ref.dtype)


def _attn_prompt(q, k, v, bias_hl, tri, tk, nsub, unroll, chunk):
    b, t, _ = q.shape
    npair = W_A // LANES
    tq = tk * nsub
    assert t % tq == 0 and nsub % unroll == 0
    return pl.pallas_call(
        functools.partial(_attn_prompt_kernel, tk=tk, nsub=nsub, unroll=unroll, chunk=chunk),
        grid=(b, npair, t // tq),
        in_specs=[pl.BlockSpec(memory_space=pltpu.SMEM),
                  pl.BlockSpec((1, tq, LANES), lambda bi, p, i: (bi, i, p)),
                  pl.BlockSpec((1, t, LANES), lambda bi, p, i: (bi, 0, p)),
                  pl.BlockSpec((1, t, LANES), lambda bi, p, i: (bi, 0, p)),
                  pl.BlockSpec((tk, tk), lambda bi, p, i: (0, 0))],
        out_specs=pl.BlockSpec((1, tq, LANES), lambda bi, p, i: (bi, i, p)),
        out_shape=jax.ShapeDtypeStruct((b, t, W_A), BF16),
        scratch_shapes=[pltpu.VMEM((t, 2 * LANES), BF16)],
        compiler_params=_cparams(("parallel", "parallel", "arbitrary")),
        name="attn_prompt",
    )(bias_hl, q, k, v, tri)


def _attn_sample_kernel(pt_ref, qbd_ref, brow_ref, kn_ref, vn_ref, tri_ref, *rest, n_group):
    del pt_ref
    pages = rest[:2 * n_group]
    o_ref, acc_ref, car_ref = rest[2 * n_group:]
    p = pl.program_id(1)
    qbd = qbd_ref[0]
    brow = brow_ref[...]
    tri = tri_ref[...]
    rows = H_A * SUBLANES
    blk = 2 * PAGE
    nblk = n_group * PAGE // blk

    @pl.when(p == 0)
    def _():
        z = _dot(qbd, kn_ref[0]) + brow
        rr = lax.broadcasted_iota(jnp.int32, (rows, PAGE), 0) % SUBLANES
        cc = lax.broadcasted_iota(jnp.int32, (rows, PAGE), 1)
        w, carry = _sb_weights(z, tri[0:PAGE, 0:PAGE], jnp.zeros((rows, 1), F32), cc < rr)
        acc_ref[...] = _dot_nt(w, vn_ref[0])
        car_ref[...] = carry

    kall = jnp.concatenate([pages[g][0, 0] for g in range(n_group)], axis=1).astype(BF16)
    vall = jnp.concatenate([pages[n_group + g][0, 0] for g in range(n_group)], axis=1).astype(BF16)
    z = _dot(qbd, kall) + brow
    spb = _softplus2(z).astype(BF16)
    cs = [_dot(spb[:, b * blk:(b + 1) * blk], tri) for b in range(nblk)]
    run = car_ref[...]
    args = [None] * nblk
    for b in reversed(range(nblk)):
        args[b] = z[:, b * blk:(b + 1) * blk] - cs[b] - run
        run = run + cs[b][:, 0:1]
    w = jnp.exp2(jnp.concatenate(args, axis=1)).astype(BF16)
    car_ref[...] = run
    acc_ref[...] += _dot_nt(w, vall)

    @pl.when(p == pl.num_programs(1) - 1)
    def _():
        acc = acc_ref[...]
        lane_head = lax.broadcasted_iota(jnp.int32, (SUBLANES, W_A), 1) // HD_A
        out = jnp.zeros((SUBLANES, W_A), F32)
        for h in range(H_A):
            out = jnp.where(lane_head == h, acc[h * SUBLANES:(h + 1) * SUBLANES, :], out)
        o_ref[0] = out.astype(o_ref.dtype)


def _attn_sample(page_table, qbd, brow, knew, vnew, tri, cache_k, cache_v, e, n_group):
    bs, n_pages = page_table.shape
    steps = n_pages // n_group
    rows = H_A * SUBLANES

    def page_spec(i):
        return pl.BlockSpec(
            (1, 1, W_A, PAGE),
            lambda b, p, pt, i=i: (e, pt[b, (steps - 1 - p) * n_group + i], 0, 0))

    grid_spec = pltpu.PrefetchScalarGridSpec(
        num_scalar_prefetch=1,
        grid=(bs, steps),
        in_specs=[pl.BlockSpec((1, rows, W_A), lambda b, p, pt: (b, 0, 0)),
                  pl.BlockSpec((rows, 1), lambda b, p, pt: (0, 0)),
                  pl.BlockSpec((1, W_A, PAGE), lambda b, p, pt: (b, 0, 0)),
                  pl.BlockSpec((1, W_A, PAGE), lambda b, p, pt: (b, 0, 0)),
                  pl.BlockSpec((2 * PAGE, 2 * PAGE), lambda b, p, pt: (0, 0))]
                 + [page_spec(i) for i in range(n_group)] * 2,
        out_specs=pl.BlockSpec((1, SUBLANES, W_A), lambda b, p, pt: (b, 0, 0)),
        scratch_shapes=[pltpu.VMEM((rows, W_A), F32), pltpu.VMEM((rows, 1), F32)],
    )
    return pl.pallas_call(
        functools.partial(_attn_sample_kernel, n_group=n_group),
        grid_spec=grid_spec,
        out_shape=jax.ShapeDtypeStruct((bs, SUBLANES, W_A), BF16),
        compiler_params=_cparams(("parallel", "arbitrary")),
        name="attn_sample",
    )(page_table, qbd, brow, knew, vnew, tri, *([cache_k] * n_group), *([cache_v] * n_group))


def _lru_kernel(xb_ref, gb_ref, h0_ref, cb_ref, wc_ref, bc_ref, wr_ref, br_ref, wi_ref, bi_ref, lam_ref,
                ob_ref, hl_ref, xs_ref, hc_ref, *, tt):
    t = pl.program_id(1)

    @pl.when(t == 0)
    def _():
        xs_ref[0:SUBLANES, :] = cb_ref[0]
        hc_ref[...] = h0_ref[0]

    x = xb_ref[0]
    xs_ref[SUBLANES:SUBLANES + tt, :] = x
    wc = wc_ref[...]
    xc = (bc_ref[...] + wc[3:4] * x + wc[2:3] * xs_ref[SUBLANES - 1:SUBLANES - 1 + tt, :]
          + wc[1:2] * xs_ref[SUBLANES - 2:SUBLANES - 2 + tt, :]
          + wc[0:1] * xs_ref[SUBLANES - 3:SUBLANES - 3 + tt, :])
    xs_ref[0:SUBLANES, :] = x[tt - SUBLANES:tt]

    xcb = xc.astype(BF16)
    r = jax.nn.sigmoid(_dot(xcb, wr_ref[...]) + br_ref[...])
    ig = jax.nn.sigmoid(_dot(xcb, wi_ref[...]) + bi_ref[...])
    nl = -lam_ref[...]
    sp_lam = jnp.maximum(nl, 0.0) + jnp.log1p(jnp.exp(-jnp.abs(nl)))
    log_a = -C_LRU * r * sp_lam
    a = jnp.exp(log_a)
    u = jnp.sqrt((1.0 + a * a) * jnp.tanh(-log_a)) * (ig * xc)
    row = lax.broadcasted_iota(jnp.int32, (tt, W_B), 0)
    u = u + jnp.where(row == 0, a * hc_ref[...], 0.0)

    d = 1
    while d < tt:
        keep = row >= d
        a_sh = jnp.where(keep, pltpu.roll(a, d, 0), 1.0)
        u_sh = jnp.where(keep, pltpu.roll(u, d, 0), 0.0)
        u = a * u_sh + u
        a = a * a_sh
        d *= 2
    h = u
    hc_ref[...] = h[tt - 1:tt]
    hl_ref[0] = h[tt - SUBLANES:tt]
    ob_ref[0] = (h * jax.nn.gelu(gb_ref[0])).astype(ob_ref.dtype)


def _lru(xb, gb, h0, cbuf8, wc, bc, wr, br, wi, bi, lam, tt):
    b, t, _ = xb.shape
    seq = lambda bi, ti: (bi, ti, 0)
    per_b = lambda bi, ti: (bi, 0, 0)
    fixed = lambda bi, ti: (0, 0)
    return pl.pallas_call(
        functools.partial(_lru_kernel, tt=tt),
        grid=(b, t // tt),
        in_specs=[pl.BlockSpec((1, tt, W_B), seq), pl.BlockSpec((1, tt, W_B), seq),
                  pl.BlockSpec((1, 1, W_B), per_b), pl.BlockSpec((1, SUBLANES, W_B), per_b),
                  pl.BlockSpec((CONV_B, W_B), fixed), pl.BlockSpec((1, W_B), fixed),
                  pl.BlockSpec((W_B, W_B), fixed), pl.BlockSpec((1, W_B), fixed),
                  pl.BlockSpec((W_B, W_B), fixed), pl.BlockSpec((1, W_B), fixed),
                  pl.BlockSpec((1, W_B), fixed)],
        out_specs=[pl.BlockSpec((1, tt, W_B), seq), pl.BlockSpec((1, SUBLANES, W_B), per_b)],
        out_shape=[jax.ShapeDtypeStruct((b, t, W_B), BF16),
                   jax.ShapeDtypeStruct((b, SUBLANES, W_B), F32)],
        scratch_shapes=[pltpu.VMEM((tt + SUBLANES, W_B), F32), pltpu.VMEM((1, W_B), F32)],
        compiler_params=_cparams(("parallel", "arbitrary")),
        name="rglru",
    )(xb, gb, h0, cbuf8, wc, bc, wr, br, wi, bi, lam)


def _even_out_kernel(x_ref, oa_ref, ob_ref, w_ref, y_ref):
    y_ref[...] = (x_ref[...] + _dot(oa_ref[...], w_ref[0:W_A, :])
                  + _dot(ob_ref[...], w_ref[W_A:W_A + W_B, :]))


def _even_out(x, oa, ob, w, tm):
    m = x.shape[0]
    row = lambda i: (i, 0)
    return pl.pallas_call(
        _even_out_kernel,
        grid=(m // tm,),
        in_specs=[pl.BlockSpec((tm, D_MODEL), row), pl.BlockSpec((tm, W_A), row),
                  pl.BlockSpec((tm, W_B), row), pl.BlockSpec((W_A + W_B, D_MODEL), lambda i: (0, 0))],
        out_specs=pl.BlockSpec((tm, D_MODEL), row),
        out_shape=jax.ShapeDtypeStruct((m, D_MODEL), F32),
        compiler_params=_cparams(("parallel",)),
        name="even_out",
    )(x, oa, ob, w)


def _odd_kernel(x_ref, g_ref, win_ref, gv_ref, wmix_ref, bmix_ref, wout_ref, y_ref, vn_ref, um_ref,
                *, tm, rblk):
    x = x_ref[...]
    xn = _rms(x, g_ref[...]).astype(BF16)
    z = jax.nn.gelu(_dot(xn, win_ref[...]))
    for h in range(H_C):
        lo, hi = h * CW_C, (h + 1) * CW_C
        vh = z[:, W_C + lo:W_C + hi]
        ms = jnp.mean(vh * vh, axis=-1, keepdims=True)
        vnh = vh * lax.rsqrt(ms + EPS) * gv_ref[:, lo:hi]
        vn_ref[:, lo:hi] = vnh
        if rblk == SUBLANES:
            v3 = vnh.reshape(tm // SUBLANES, SUBLANES, CW_C)
            mixed = wmix_ref[0, :, lo:hi][None] * v3 + bmix_ref[:, lo:hi][None]
            for kk in range(1, SUBLANES):
                mixed = mixed + wmix_ref[kk, :, lo:hi][None] * pltpu.roll(v3, kk, 1)
            um_ref[:, lo:hi] = (z[:, lo:hi] * mixed.reshape(tm, CW_C)).astype(BF16)
        else:
            vnb = vnh.astype(BF16)
            wm = wmix_ref[h]
            for c in range(tm // rblk):
                r0, r1 = c * rblk, (c + 1) * rblk
                mixed = _dot(wm, vnb[r0:r1]) + bmix_ref[:, lo:hi]
                um_ref[r0:r1, lo:hi] = (z[r0:r1, lo:hi] * mixed).astype(BF16)
    y_ref[...] = x + _dot(um_ref[...], wout_ref[...])


def _odd_mixer(x, g, win, gv, wmix, bmix, wout, tm, rblk):
    m = x.shape[0]
    row = lambda i: (i, 0)
    fixed = lambda i: (0, 0)
    return pl.pallas_call(
        functools.partial(_odd_kernel, tm=tm, rblk=rblk),
        grid=(m // tm,),
        in_specs=[pl.BlockSpec((tm, D_MODEL), row), pl.BlockSpec((1, D_MODEL), fixed),
                  pl.BlockSpec((D_MODEL, 2 * W_C), fixed), pl.BlockSpec((1, W_C), fixed),
                  pl.BlockSpec(wmix.shape, lambda i: (0, 0, 0)),
                  pl.BlockSpec((rblk, W_C), fixed), pl.BlockSpec((W_C, D_MODEL), fixed)],
        out_specs=[pl.BlockSpec((tm, D_MODEL), row), pl.BlockSpec((tm, W_C), row)],
        out_shape=[jax.ShapeDtypeStruct((m, D_MODEL), F32), jax.ShapeDtypeStruct((m, W_C), F32)],
        scratch_shapes=[pltpu.VMEM((tm, W_C), BF16)],
        compiler_params=_cparams(("parallel",)),
        name="odd_mixer",
    )(x, g, win, gv, wmix, bmix, wout)


def _ffn_kernel(x_ref, g_ref, init_ref, wg_ref, wu_ref, wc_ref, bc_ref, wd_ref,
                y_ref, gt_ref, xn_ref, car_ref, *, tm, tiles_per_seq, short_seq):
    m = pl.program_id(0)
    j = pl.program_id(1)

    @pl.when(j == 0)
    def _():
        x = x_ref[...]
        xn_ref[...] = _rms(x, g_ref[...]).astype(BF16)
        y_ref[...] = x

    if not short_seq:
        @pl.when(m % tiles_per_seq == 0)
        def _():
            car_ref[j] = init_ref[0]

    xn = xn_ref[...]
    g = _dot(xn, wg_ref[...])
    u = _dot(xn, wu_ref[...])
    g1 = pltpu.roll(g, 1, 0)
    g2 = pltpu.roll(g, 2, 0)
    if short_seq:
        hist = init_ref[...]
        r8 = lax.broadcasted_iota(jnp.int32, g.shape, 0) % SUBLANES
        p1 = jnp.where(r8 == 0, pltpu.roll(hist, tm - 1, 0), g1)
        p2 = jnp.where(r8 < 2, hist, g2)
        gt_ref[...] = g
    else:
        prev = car_ref[j]
        r8 = lax.broadcasted_iota(jnp.int32, prev.shape, 0)
        p1 = jnp.concatenate([jnp.where(r8 < 1, pltpu.roll(prev, 1, 0), g1[0:SUBLANES]), g1[SUBLANES:]], axis=0)
        p2 = jnp.concatenate([jnp.where(r8 < 2, pltpu.roll(prev, 2, 0), g2[0:SUBLANES]), g2[SUBLANES:]], axis=0)
        tail = g[tm - SUBLANES:tm]
        car_ref[j] = tail
        gt_ref[0] = tail
    wc = wc_ref[...]
    gc = bc_ref[...] + wc[2:3] * g + wc[1:2] * p1 + wc[0:1] * p2
    hid = (jax.nn.gelu(gc) * u).astype(BF16)
    y_ref[...] += _dot(hid, wd_ref[...])


def _ffn(x, g, init, wg, wu, wc, bc, wd, layer, tm, tf, seq_len):
    m = x.shape[0]
    nff = D_FF // tf
    short_seq = seq_len < tm
    if short_seq:
        assert seq_len == SUBLANES and m == tm
        tiles_per_seq = 1
        init_spec = pl.BlockSpec((tm, tf), lambda i, j: (i, j))
        gt_spec = pl.BlockSpec((tm, tf), lambda i, j: (i, j))
        gt_shape = jax.ShapeDtypeStruct((m, D_FF), F32)
    else:
        assert seq_len % tm == 0
        tiles_per_seq = seq_len // tm
        init_spec = pl.BlockSpec((1, SUBLANES, tf), lambda i, j: (i // tiles_per_seq, 0, j))
        gt_spec = pl.BlockSpec((1, SUBLANES, tf), lambda i, j: (i, 0, j))
        gt_shape = jax.ShapeDtypeStruct((m // tm, SUBLANES, D_FF), F32)
    return pl.pallas_call(
        functools.partial(_ffn_kernel, tm=tm, tiles_per_seq=tiles_per_seq, short_seq=short_seq),
        grid=(m // tm, nff),
        in_specs=[pl.BlockSpec((tm, D_MODEL), lambda i, j: (i, 0)),
                  pl.BlockSpec((1, D_MODEL), lambda i, j: (0, 0)),
                  init_spec,
                  pl.BlockSpec((None, D_MODEL, tf), lambda i, j: (layer, 0, j)),
                  pl.BlockSpec((None, D_MODEL, tf), lambda i, j: (layer, 0, j)),
                  pl.BlockSpec((None, CONV_F, tf), lambda i, j: (layer, 0, j)),
                  pl.BlockSpec((None, 1, tf), lambda i, j: (layer, 0, j)),
                  pl.BlockSpec((None, tf, D_MODEL), lambda i, j: (layer, j, 0))],
        out_specs=[pl.BlockSpec((tm, D_MODEL), lambda i, j: (i, 0)), gt_spec],
        out_shape=[jax.ShapeDtypeStruct((m, D_MODEL), F32), gt_shape],
        scratch_shapes=[pltpu.VMEM((tm, D_MODEL), BF16), pltpu.VMEM((nff, SUBLANES, tf), F32)],
        compiler_params=_cparams(("arbitrary", "arbitrary")),
        name="conv_ffn",
    )(x, g, init, wg, wu, wc, bc, wd)


def _block_diag(w):
    n, c, d = w.shape
    eye = jnp.eye(n, dtype=w.dtype)
    return (eye[:, None, :, None] * w[:, :, None, :]).reshape(n * c, n * d)


def _suffix_tri(n):
    j = np.arange(n)[:, None]
    s = np.arange(n)[None, :]
    return jnp.asarray((j >= s).astype(np.float32), dtype=BF16)


def kernel(x_prompt, x_sample, cache_k, cache_v, state_lru_h, state_lru_conv, state_ffn_conv, page_table,
           g_mix, g_ffn, w_in_even, g_q, g_k, sb_bias, w_conv_lru, b_conv_lru, w_rgate, b_rgate, w_igate,
           b_igate, lru_lambda, w_out_even, w_in_odd, g_v, w_spatial, b_spatial, w_out_odd,
           w_gate, w_up, w_ffn_conv, b_ffn_conv, w_down):
    bp, tp, _ = x_prompt.shape
    bs, ts, _ = x_sample.shape
    depth = g_mix.shape[0]
    n_pool = cache_k.shape[1]
    mp, ms = bp * tp, bs * ts
    assert ts == SUBLANES and tp % 512 == 0

    tm_p, tm_s = 512, ms
    tm_f = 1024
    tk, nsub, unroll, chunk = 256, 4, 4, 1024
    tt_p = 256
    tf = 1024
    n_group = 16

    yp = x_prompt.reshape(mp, D_MODEL)
    ys = x_sample.reshape(ms, D_MODEL)

    head_mean = jnp.asarray(np.kron(np.eye(H_A), np.full((HD_A, HD_A), 1.0 / HD_A)), dtype=BF16)
    tri = _suffix_tri(tk)
    tri_s = _suffix_tri(2 * PAGE)
    ck = jnp.transpose(cache_k, (0, 1, 3, 4, 2)).reshape(cache_k.shape[0], n_pool, W_A, PAGE)
    cv = jnp.transpose(cache_v, (0, 1, 3, 4, 2)).reshape(cache_v.shape[0], n_pool, W_A, PAGE)
    row_head = np.arange(H_A * SUBLANES)[:, None] // SUBLANES
    lane_head = np.arange(W_A)[None, :] // HD_A
    qbd_mask = jnp.asarray(row_head == lane_head)

    ffn_w = (w_gate.astype(BF16), w_up.astype(BF16), w_ffn_conv, b_ffn_conv.reshape(depth, 1, D_FF),
             w_down.astype(BF16))
    outs = {n: [] for n in ("kp", "vp", "ks", "vs", "hp", "hs", "cp", "cs", "chp", "chs", "fp", "fs")}
    last_chunk_start = ((tp - 1) // CHUNK) * CHUNK

    for l in range(depth):
        gm = g_mix[l].reshape(1, D_MODEL)
        if l % 2 == 0:
            e = l // 2
            w_in = w_in_even[e].astype(BF16)
            gq = (jnp.tile(g_q[e], H_A) * (HD_A ** -0.5 * LOG2E)).reshape(1, W_A)
            gk = jnp.tile(g_k[e], H_A).reshape(1, W_A)
            bias2 = sb_bias[e] * LOG2E
            bias_hi = bias2.astype(BF16).astype(F32)
            bias_hl = jnp.stack([bias_hi, (bias2 - bias_hi).astype(BF16).astype(F32)])
            wr = _block_diag(w_rgate[e]).astype(BF16)
            wi = _block_diag(w_igate[e]).astype(BF16)
            br = b_rgate[e].reshape(1, W_B)
            bi = b_igate[e].reshape(1, W_B)
            lam = lru_lambda[e].reshape(1, W_B)
            bc = b_conv_lru[e].reshape(1, W_B)
            w_out = w_out_even[e].astype(BF16)
            lru_w = (w_conv_lru[e], bc, wr, br, wi, bi, lam)

            q, kt, vt, kb, vb, xb, gb = _even_in(yp, gm, w_in, head_mean, gq, gk, tm_p, tp)
            oa = _attn_prompt(q.reshape(bp, tp, W_A), kb.reshape(bp, tp, W_A), vb.reshape(bp, tp, W_A),
                              bias_hl, tri, tk, nsub, unroll, chunk)
            xb3 = xb.reshape(bp, tp, W_B)
            ob, hl = _lru(xb3, gb.reshape(bp, tp, W_B), jnp.zeros((bp, 1, W_B), F32),
                          jnp.zeros((bp, SUBLANES, W_B), F32), *lru_w, tt_p)
            yp = _even_out(yp, oa.reshape(mp, W_A), ob.reshape(mp, W_B), w_out, tm_p)
            to_thd = lambda a: jnp.transpose(a.reshape(bp, H_A, HD_A, tp), (0, 3, 1, 2))
            outs["kp"].append(to_thd(kt))
            outs["vp"].append(to_thd(vt))
            outs["hp"].append(hl[:, SUBLANES - 1])
            outs["cp"].append(xb3[:, tp - (CONV_B - 1):])

            q, k, v, kb, vb, xb, gb = _even_in(ys, gm, w_in, head_mean, gq, gk, tm_s, None)
            qbd = jnp.where(qbd_mask, jnp.tile(q.reshape(bs, ts, W_A), (1, H_A, 1)), jnp.zeros((), BF16))
            brow = jnp.repeat(bias2, SUBLANES).reshape(H_A * SUBLANES, 1)
            pad = ((0, 0), (0, 0), (0, PAGE - ts))
            knew = jnp.pad(jnp.swapaxes(kb.reshape(bs, ts, W_A), 1, 2), pad)
            vnew = jnp.pad(jnp.swapaxes(vb.reshape(bs, ts, W_A), 1, 2), pad)
            oa = _attn_sample(page_table, qbd, brow, knew, vnew, tri_s, ck, cv, e, n_group)
            xb3 = xb.reshape(bs, ts, W_B)
            cbuf8 = jnp.pad(state_lru_conv[e], ((0, 0), (SUBLANES - (CONV_B - 1), 0), (0, 0)))
            ob, hl = _lru(xb3, gb.reshape(bs, ts, W_B), state_lru_h[e].reshape(bs, 1, W_B), cbuf8,
                          *lru_w, ts)
            ys = _even_out(ys, oa.reshape(ms, W_A), ob.reshape(ms, W_B), w_out, tm_s)
            outs["ks"].append(k.reshape(bs, ts, H_A, HD_A))
            outs["vs"].append(v.reshape(bs, ts, H_A, HD_A))
            outs["hs"].append(hl[:, SUBLANES - 1])
            outs["cs"].append(xb3[:, ts - (CONV_B - 1):])
        else:
            o = l // 2
            w_in = w_in_odd[o].astype(BF16)
            gv = g_v[o].reshape(1, W_C)
            w_out = w_out_odd[o].astype(BF16)
            w_tril = jnp.tril(w_spatial[o])
            wmix_p = w_tril.astype(BF16)
            bmix_p = jnp.repeat(b_spatial[o].T, CW_C, axis=1)
            taps = jnp.stack([jnp.pad(jnp.diagonal(w_tril[:, :ts, :ts], offset=-k, axis1=1, axis2=2),
                                      ((0, 0), (k, 0))) for k in range(ts)])
            wmix_s = jnp.repeat(jnp.transpose(taps, (0, 2, 1)), CW_C, axis=2)
            bmix_s = bmix_p[:ts]

            yp, vn = _odd_mixer(yp, gm, w_in, gv, wmix_p, bmix_p, w_out, tm_p, CHUNK)
            outs["chp"].append(vn.reshape(bp, tp, W_C)[:, last_chunk_start:])
            ys, vn = _odd_mixer(ys, gm, w_in, gv, wmix_s, bmix_s, w_out, tm_s, ts)
            outs["chs"].append(vn.reshape(bs, ts, W_C))

        gf = g_ffn[l].reshape(1, D_MODEL)
        yp, gt = _ffn(yp, gf, jnp.zeros((bp, SUBLANES, D_FF), F32), *ffn_w, l, tm_f, tf, tp)
        tiles = tp // tm_f
        outs["fp"].append(gt.reshape(bp, tiles, SUBLANES, D_FF)[:, tiles - 1, SUBLANES - (CONV_F - 1):])
        hist = jnp.pad(state_ffn_conv[l], ((0, 0), (0, SUBLANES - (CONV_F - 1)), (0, 0))).reshape(ms, D_FF)
        ys, gt = _ffn(ys, gf, hist, *ffn_w, l, tm_s, tf, ts)
        outs["fs"].append(gt.reshape(bs, ts, D_FF)[:, ts - (CONV_F - 1):])

    st = lambda n: jnp.stack(outs[n])
    return (yp.reshape(bp, tp, D_MODEL), ys.reshape(bs, ts, D_MODEL),
            st("kp"), st("vp"), st("ks"), st("vs"), st("hp"), st("hs"), st("cp"), st("cs"),
            st("chp"), st("chs"), st("fp"), st("fs"))
```

```python
import functools
import math

import numpy as np
import jax
import jax.numpy as jnp
from jax import lax
from jax.experimental import pallas as pl
from jax.experimental.pallas import tpu as pltpu

F32 = jnp.float32
BF16 = jnp.bfloat16

D_MODEL = 1024
H_A, HD_A = 8, 64
W_A = H_A * HD_A
H_B, BW_B = 8, 64
W_B = H_B * BW_B
CONV_B = 4
C_LRU = 8.0
H_C, CW_C = 8, 128
W_C = H_C * CW_C
CHUNK = 128
D_FF = 3 * D_MODEL
CONV_F = 3
PAGE = 128
EPS = 1e-6
D_IN_EVEN = 3 * W_A + 2 * W_B
LOG2E = math.log2(math.e)

LANES = 128
SUBLANES = 8
VMEM_LIMIT = 56 * 1024 * 1024


def _cparams(sem):
    return pltpu.CompilerParams(dimension_semantics=sem, vmem_limit_bytes=VMEM_LIMIT)


def _rms(x, g):
    ms = jnp.mean(x * x, axis=-1, keepdims=True)
    return x * lax.rsqrt(ms + EPS) * g


def _dot(a, b):
    return jnp.dot(a, b, preferred_element_type=F32)


def _dot_nt(a, b):
    return lax.dot_general(a, b, (((1,), (1,)), ((), ())), preferred_element_type=F32)


def _even_in_kernel(x_ref, g_ref, w_ref, pm_ref, gq_ref, gk_ref,
                    q_ref, k_ref, v_ref, kb_ref, vb_ref, xb_ref, gb_ref, *, kv_transposed):
    xn = _rms(x_ref[...], g_ref[...]).astype(BF16)
    p = _dot(xn, w_ref[...])
    pm = pm_ref[...]

    def head_norm(t, g):
        sq = t * t
        hi = sq.astype(BF16)
        lo = (sq - hi.astype(F32)).astype(BF16)
        ms = _dot(hi, pm) + _dot(lo, pm)
        return t * lax.rsqrt(ms + EPS) * g

    q = head_norm(p[:, 0:W_A], gq_ref[...])
    k = head_norm(p[:, W_A:2 * W_A], gk_ref[...])
    v = p[:, 2 * W_A:3 * W_A]
    q_ref[...] = q.astype(BF16)
    if kv_transposed:
        k_ref[0] = k.T
        v_ref[0] = v.T
    else:
        k_ref[...] = k
        v_ref[...] = v
    kb_ref[...] = k.astype(BF16)
    vb_ref[...] = v.astype(BF16)
    xb_ref[...] = p[:, 3 * W_A:3 * W_A + W_B]
    gb_ref[...] = p[:, 3 * W_A + W_B:]


def _even_in(x, g, w, pm, gq, gk, tm, seq_len):
    m = x.shape[0]
    row = lambda i: (i, 0)
    fixed = lambda i: (0, 0)
    f32o = jax.ShapeDtypeStruct((m, W_A), F32)
    bfo = jax.ShapeDtypeStruct((m, W_A), BF16)
    blk = pl.BlockSpec((tm, W_A), row)
    kv_shape, kv_blk = f32o, blk
    if seq_len is not None:
        tps = seq_len // tm
        kv_shape = jax.ShapeDtypeStruct((m // seq_len, W_A, seq_len), F32)
        kv_blk = pl.BlockSpec((1, W_A, tm), lambda i: (i // tps, 0, i % tps))
    return pl.pallas_call(
        functools.partial(_even_in_kernel, kv_transposed=seq_len is not None),
        grid=(m // tm,),
        in_specs=[pl.BlockSpec((tm, D_MODEL), row), pl.BlockSpec((1, D_MODEL), fixed),
                  pl.BlockSpec((D_MODEL, D_IN_EVEN), fixed), pl.BlockSpec((W_A, W_A), fixed),
                  pl.BlockSpec((1, W_A), fixed), pl.BlockSpec((1, W_A), fixed)],
        out_specs=[blk, kv_blk, kv_blk, blk, blk, blk, blk],
        out_shape=[bfo, kv_shape, kv_shape, bfo, bfo, f32o, f32o],
        compiler_params=_cparams(("parallel",)),
        name="even_in",
    )(x, g, w, pm, gq, gk)


def _softplus2(z):
    return jnp.maximum(z, jnp.log2(1.0 + jnp.exp2(jnp.minimum(z, 100.0))))


def _sb_weights(z, tri, carry, mask):
    sp = _softplus2(z)
    if mask is not None:
        sp = jnp.where(mask, sp, 0.0)
    c = _dot(sp.astype(BF16), tri)
    w = jnp.exp2(z - c - carry)
    if mask is not None:
        w = jnp.where(mask, w, 0.0)
    return w.astype(BF16), carry + c[:, 0:1]


def _attn_prompt_kernel(bias_ref, q_ref, k_ref, v_ref, tri_ref, o_ref, ka_ref, *, tk, nsub, unroll, chunk):
    pair = pl.program_id(1)
    i = pl.program_id(2)
    t = k_ref.shape[1]
    rows = 2 * nsub * tk

    @pl.when(i == 0)
    def _():
        ka_ref[:, 0:LANES] = k_ref[0]
        lane_t = lax.broadcasted_iota(jnp.int32, (t, LANES), 1)
        ka_ref[:, LANES:2 * LANES] = jnp.where(lane_t < 2, 1.0, 0.0).astype(BF16)

    q = q_ref[0]
    tri = tri_ref[...]
    lane = lax.broadcasted_iota(jnp.int32, (tk, LANES), 1)
    first = lane < HD_A
    zero = jnp.zeros((tk, LANES), BF16)
    ext = []
    for h in range(2):
        hi = bias_ref[0, 2 * pair + h]
        lo = bias_ref[1, 2 * pair + h]
        ext.append(jnp.where(lane == 0, hi, jnp.where(lane == 1, lo, 0.0)).astype(BF16))
    chains = []
    for s in range(nsub):
        qs = q[s * tk:(s + 1) * tk]
        chains.append(jnp.concatenate([jnp.where(first, qs, zero), ext[0]], axis=1))
        chains.append(jnp.concatenate([jnp.where(first, zero, qs), ext[1]], axis=1))
    qa = jnp.concatenate(chains, axis=0)

    def sweep(j_hi, nblk, qa_r, carry, acc, diag_rows):
        r = qa_r.shape[0]
        ch = min(chunk, r)
        for u in range(nblk):
            lo = pl.multiple_of((j_hi - u) * tk, tk)
            kj = ka_ref[pl.ds(lo, tk), :]
            vj = v_ref[0, pl.ds(lo, tk), :]
            cs, accs = [], []
            for c0 in range(0, r, ch):
                sl = slice(c0, c0 + ch)
                z = _dot_nt(qa_r[sl], kj)
                mask = None
                if diag_rows and u == 0 and c0 < diag_rows:
                    rr = lax.broadcasted_iota(jnp.int32, (ch, tk), 0) + c0
                    cc = lax.broadcasted_iota(jnp.int32, (ch, tk), 1)
                    mask = (rr >= diag_rows) | (cc < (rr % tk))
                w, c2 = _sb_weights(z, tri, carry[sl], mask)
                cs.append(c2)
                accs.append(acc[sl] + _dot(w, vj))
            carry = cs[0] if len(cs) == 1 else jnp.concatenate(cs, axis=0)
            acc = accs[0] if len(accs) == 1 else jnp.concatenate(accs, axis=0)
        return carry, acc

    carry = jnp.zeros((rows, 1), F32)
    acc = jnp.zeros((rows, LANES), F32)
    for m in reversed(range(nsub)):
        r0 = m * 2 * tk
        c_m, a_m = sweep(nsub * i + m, 1, qa[r0:], carry[r0:], acc[r0:], 2 * tk)
        carry = jnp.concatenate([carry[:r0], c_m], axis=0) if r0 else c_m
        acc = jnp.concatenate([acc[:r0], a_m], axis=0) if r0 else a_m

    nloop = nsub * i
    carry, acc = lax.fori_loop(
        0, nloop // unroll,
        lambda n, st: sweep(nloop - 1 - unroll * n, unroll, qa, st[0], st[1], 0), (carry, acc))
    for s in range(nsub):
        a0 = acc[(2 * s) * tk:(2 * s + 1) * tk]
        a1 = acc[(2 * s + 1) * tk:(2 * s + 2) * tk]
        o_ref[0, s * tk:(s + 1) * tk, :] = jnp.where(first, a0, a1).astype(o_ref.dtype)


def _attn_prompt(q, k, v, bias_hl, tri, tk, nsub, unroll, chunk):
    b, t, _ = q.shape
    npair = W_A // LANES
    tq = tk * nsub
    assert t % tq == 0 and nsub % unroll == 0
    return pl.pallas_call(
        functools.partial(_attn_prompt_kernel, tk=tk, nsub=nsub, unroll=unroll, chunk=chunk),
        grid=(b, npair, t // tq),
        in_specs=[pl.BlockSpec(memory_space=pltpu.SMEM),
                  pl.BlockSpec((1, tq, LANES), lambda bi, p, i: (bi, i, p)),
                  pl.BlockSpec((1, t, LANES), lambda bi, p, i: (bi, 0, p)),
                  pl.BlockSpec((1, t, LANES), lambda bi, p, i: (bi, 0, p)),
                  pl.BlockSpec((tk, tk), lambda bi, p, i: (0, 0))],
        out_specs=pl.BlockSpec((1, tq, LANES), lambda bi, p, i: (bi, i, p)),
        out_shape=jax.ShapeDtypeStruct((b, t, W_A), BF16),
        scratch_shapes=[pltpu.VMEM((t, 2 * LANES), BF16)],
        compiler_params=_cparams(("parallel", "parallel", "arbitrary")),
        name="attn_prompt",
    )(bias_hl, q, k, v, tri)


def _attn_sample_kernel(pt_ref, qbd_ref, brow_ref, kn_ref, vn_ref, tri_ref, *rest, n_group):
    del pt_ref
    pages = rest[:2 * n_group]
    o_ref, acc_ref, car_ref = rest[2 * n_group:]
    p = pl.program_id(1)
    qbd = qbd_ref[0]
    brow = brow_ref[...]
    tri = tri_ref[...]
    rows = H_A * SUBLANES
    blk = 2 * PAGE
    nblk = n_group * PAGE // blk

    @pl.when(p == 0)
    def _():
        z = _dot(qbd, kn_ref[0]) + brow
        rr = lax.broadcasted_iota(jnp.int32, (rows, PAGE), 0) % SUBLANES
        cc = lax.broadcasted_iota(jnp.int32, (rows, PAGE), 1)
        w, carry = _sb_weights(z, tri[0:PAGE, 0:PAGE], jnp.zeros((rows, 1), F32), cc < rr)
        acc_ref[...] = _dot_nt(w, vn_ref[0])
        car_ref[...] = carry

    kall = jnp.concatenate([pages[g][0, 0] for g in range(n_group)], axis=1).astype(BF16)
    vall = jnp.concatenate([pages[n_group + g][0, 0] for g in range(n_group)], axis=1).astype(BF16)
    z = _dot(qbd, kall) + brow
    spb = _softplus2(z).astype(BF16)
    cs = [_dot(spb[:, b * blk:(b + 1) * blk], tri) for b in range(nblk)]
    run = car_ref[...]
    args = [None] * nblk
    for b in reversed(range(nblk)):
        args[b] = z[:, b * blk:(b + 1) * blk] - cs[b] - run
        run = run + cs[b][:, 0:1]
    w = jnp.exp2(jnp.concatenate(args, axis=1)).astype(BF16)
    car_ref[...] = run
    acc_ref[...] += _dot_nt(w, vall)

    @pl.when(p == pl.num_programs(1) - 1)
    def _():
        acc = acc_ref[...]
        lane_head = lax.broadcasted_iota(jnp.int32, (SUBLANES, W_A), 1) // HD_A
        out = jnp.zeros((SUBLANES, W_A), F32)
        for h in range(H_A):
            out = jnp.where(lane_head == h, acc[h * SUBLANES:(h + 1) * SUBLANES, :], out)
        o_ref[0] = out.astype(o_ref.dtype)


def _attn_sample(page_table, qbd, brow, knew, vnew, tri, cache_k, cache_v, e, n_group):
    bs, n_pages = page_table.shape
    steps = n_pages // n_group
    rows = H_A * SUBLANES

    def page_spec(i):
        return pl.BlockSpec(
            (1, 1, W_A, PAGE),
            lambda b, p, pt, i=i: (e, pt[b, (steps - 1 - p) * n_group + i], 0, 0))

    grid_spec = pltpu.PrefetchScalarGridSpec(
        num_scalar_prefetch=1,
        grid=(bs, steps),
        in_specs=[pl.BlockSpec((1, rows, W_A), lambda b, p, pt: (b, 0, 0)),
                  pl.BlockSpec((rows, 1), lambda b, p, pt: (0, 0)),
                  pl.BlockSpec((1, W_A, PAGE), lambda b, p, pt: (b, 0, 0)),
                  pl.BlockSpec((1, W_A, PAGE), lambda b, p, pt: (b, 0, 0)),
                  pl.BlockSpec((2 * PAGE, 2 * PAGE), lambda b, p, pt: (0, 0))]
                 + [page_spec(i) for i in range(n_group)] * 2,
        out_specs=pl.BlockSpec((1, SUBLANES, W_A), lambda b, p, pt: (b, 0, 0)),
        scratch_shapes=[pltpu.VMEM((rows, W_A), F32), pltpu.VMEM((rows, 1), F32)],
    )
    return pl.pallas_call(
        functools.partial(_attn_sample_kernel, n_group=n_group),
        grid_spec=grid_spec,
        out_shape=jax.ShapeDtypeStruct((bs, SUBLANES, W_A), BF16),
        compiler_params=_cparams(("parallel", "arbitrary")),
        name="attn_sample",
    )(page_table, qbd, brow, knew, vnew, tri, *([cache_k] * n_group), *([cache_v] * n_group))


def _lru_kernel(xb_ref, gb_ref, h0_ref, cb_ref, res_ref, oa_ref, wc_ref, bc_ref, wr_ref, br_ref, wi_ref, bi_ref,
                lam_ref, wo_ref, y_ref, hl_ref, xp_ref, hc_ref, *, tt, short_seq):
    t = pl.program_id(1)

    if not short_seq:
        @pl.when(t == 0)
        def _():
            xp_ref[...] = cb_ref[0]
            hc_ref[...] = h0_ref[0]

    x = xb_ref[0]
    row = lax.broadcasted_iota(jnp.int32, (tt, W_B), 0)
    r8 = row % SUBLANES
    wc = wc_ref[...]
    xc = bc_ref[...] + wc[3:4] * x
    if short_seq:
        hist = cb_ref[0]
        for k in range(1, CONV_B):
            shift = (tt + k - (CONV_B - 1)) % tt
            hist_k = pltpu.roll(hist, shift, 0) if shift else hist
            xc = xc + wc[CONV_B - 1 - k:CONV_B - k] * jnp.where(r8 < k, hist_k, pltpu.roll(x, k, 0))
    else:
        prev = xp_ref[...]
        top = lax.broadcasted_iota(jnp.int32, (SUBLANES, W_B), 0)
        for k in range(1, CONV_B):
            xk = pltpu.roll(x, k, 0)
            prev_k = jnp.concatenate([jnp.where(top < k, pltpu.roll(prev, k, 0), xk[0:SUBLANES]), xk[SUBLANES:]],
                                     axis=0)
            xc = xc + wc[CONV_B - 1 - k:CONV_B - k] * prev_k
        xp_ref[...] = x[tt - SUBLANES:tt]

    xcb = xc.astype(BF16)
    r = jax.nn.sigmoid(_dot(xcb, wr_ref[...]) + br_ref[...])
    ig = jax.nn.sigmoid(_dot(xcb, wi_ref[...]) + bi_ref[...])
    nl = -lam_ref[...]
    sp_lam = jnp.maximum(nl, 0.0) + jnp.log1p(jnp.exp(-jnp.abs(nl)))
    log_a = -C_LRU * r * sp_lam
    a = jnp.exp(log_a)
    u = jnp.sqrt((1.0 + a * a) * jnp.tanh(-log_a)) * (ig * xc)

    grp = (tt // SUBLANES, SUBLANES, W_B)
    a, u = a.reshape(grp), u.reshape(grp)
    r8g = lax.broadcasted_iota(jnp.int32, grp, 1)
    for d in (1, 2, 4):
        keep = r8g >= d
        a_sh = jnp.where(keep, pltpu.roll(a, d, 1), 1.0)
        u_sh = jnp.where(keep, pltpu.roll(u, d, 1), 0.0)
        u = a * u_sh + u
        a = a * a_sh
    a, u = a.reshape(tt, W_B), u.reshape(tt, W_B)
    if short_seq:
        h = a * h0_ref[0] + u
        hl_ref[0] = h
    else:
        h_in = hc_ref[...]
        parts = []
        for g in range(tt // SUBLANES):
            sl = slice(g * SUBLANES, (g + 1) * SUBLANES)
            h_g = a[sl] * h_in + u[sl]
            h_in = h_g[SUBLANES - 1:SUBLANES]
            parts.append(h_g)
        h = jnp.concatenate(parts, axis=0)
        hc_ref[...] = h_in
        hl_ref[0] = parts[-1]
    ob = (h * jax.nn.gelu(gb_ref[0])).astype(BF16)
    y_ref[0] = (res_ref[0] + _dot(oa_ref[0], wo_ref[0:W_A, :]) + _dot(ob, wo_ref[W_A:W_A + W_B, :]))


def _lru_out(xb, gb, h0, cbuf, res, oa, wc, bc, wr, br, wi, bi, lam, wo, tt, short_seq):
    b, t, _ = xb.shape
    seq = lambda bi, ti: (bi, ti, 0)
    per_b = lambda bi, ti: (bi, 0, 0)
    fixed = lambda bi, ti: (0, 0)
    if short_seq:
        assert t == tt
        h0_spec, cb_spec = pl.BlockSpec((1, tt, W_B), seq), pl.BlockSpec((1, tt, W_B), seq)
        hl_spec, hl_shape = pl.BlockSpec((1, tt, W_B), seq), jax.ShapeDtypeStruct((b, t, W_B), F32)
    else:
        h0_spec, cb_spec = pl.BlockSpec((1, 1, W_B), per_b), pl.BlockSpec((1, SUBLANES, W_B), per_b)
        hl_spec, hl_shape = pl.BlockSpec((1, SUBLANES, W_B), per_b), jax.ShapeDtypeStruct((b, SUBLANES, W_B), F32)
    return pl.pallas_call(
        functools.partial(_lru_kernel, tt=tt, short_seq=short_seq),
        grid=(b, t // tt),
        in_specs=[pl.BlockSpec((1, tt, W_B), seq), pl.BlockSpec((1, tt, W_B), seq), h0_spec, cb_spec,
                  pl.BlockSpec((1, tt, D_MODEL), seq), pl.BlockSpec((1, tt, W_A), seq),
                  pl.BlockSpec((CONV_B, W_B), fixed), pl.BlockSpec((1, W_B), fixed),
                  pl.BlockSpec((W_B, W_B), fixed), pl.BlockSpec((1, W_B), fixed),
                  pl.BlockSpec((W_B, W_B), fixed), pl.BlockSpec((1, W_B), fixed),
                  pl.BlockSpec((1, W_B), fixed), pl.BlockSpec((W_A + W_B, D_MODEL), fixed)],
        out_specs=[pl.BlockSpec((1, tt, D_MODEL), seq), hl_spec],
        out_shape=[jax.ShapeDtypeStruct((b, t, D_MODEL), F32), hl_shape],
        scratch_shapes=[pltpu.VMEM((SUBLANES, W_B), F32), pltpu.VMEM((1, W_B), F32)],
        compiler_params=_cparams(("parallel", "arbitrary")),
        name="rglru_out",
    )(xb, gb, h0, cbuf, res, oa, wc, bc, wr, br, wi, bi, lam, wo)


def _odd_kernel(x_ref, g_ref, win_ref, gv_ref, wmix_ref, bmix_ref, wout_ref, y_ref, vn_ref, um_ref,
                *, tm, rblk):
    x = x_ref[...]
    xn = _rms(x, g_ref[...]).astype(BF16)
    z = jax.nn.gelu(_dot(xn, win_ref[...]))
    for h in range(H_C):
        lo, hi = h * CW_C, (h + 1) * CW_C
        vh = z[:, W_C + lo:W_C + hi]
        ms = jnp.mean(vh * vh, axis=-1, keepdims=True)
        vnh = vh * lax.rsqrt(ms + EPS) * gv_ref[:, lo:hi]
        vn_ref[:, lo:hi] = vnh
        if rblk == SUBLANES:
            v3 = vnh.reshape(tm // SUBLANES, SUBLANES, CW_C)
            mixed = wmix_ref[0, :, lo:hi][None] * v3 + bmix_ref[:, lo:hi][None]
            for kk in range(1, SUBLANES):
                mixed = mixed + wmix_ref[kk, :, lo:hi][None] * pltpu.roll(v3, kk, 1)
            um_ref[:, lo:hi] = (z[:, lo:hi] * mixed.reshape(tm, CW_C)).astype(BF16)
        else:
            vnb = vnh.astype(BF16)
            wm = wmix_ref[h]
            for c in range(tm // rblk):
                r0, r1 = c * rblk, (c + 1) * rblk
                mixed = _dot(wm, vnb[r0:r1]) + bmix_ref[:, lo:hi]
                um_ref[r0:r1, lo:hi] = (z[r0:r1, lo:hi] * mixed).astype(BF16)
    y_ref[...] = x + _dot(um_ref[...], wout_ref[...])


def _odd_mixer(x, g, win, gv, wmix, bmix, wout, tm, rblk):
    m = x.shape[0]
    row = lambda i: (i, 0)
    fixed = lambda i: (0, 0)
    return pl.pallas_call(
        functools.partial(_odd_kernel, tm=tm, rblk=rblk),
        grid=(m // tm,),
        in_specs=[pl.BlockSpec((tm, D_MODEL), row), pl.BlockSpec((1, D_MODEL), fixed),
                  pl.BlockSpec((D_MODEL, 2 * W_C), fixed), pl.BlockSpec((1, W_C), fixed),
                  pl.BlockSpec(wmix.shape, lambda i: (0, 0, 0)),
                  pl.BlockSpec((rblk, W_C), fixed), pl.BlockSpec((W_C, D_MODEL), fixed)],
        out_specs=[pl.BlockSpec((tm, D_MODEL), row), pl.BlockSpec((tm, W_C), row)],
        out_shape=[jax.ShapeDtypeStruct((m, D_MODEL), F32), jax.ShapeDtypeStruct((m, W_C), F32)],
        scratch_shapes=[pltpu.VMEM((tm, W_C), BF16)],
        compiler_params=_cparams(("parallel",)),
        name="odd_mixer",
    )(x, g, win, gv, wmix, bmix, wout)


def _ffn_kernel(x_ref, g_ref, init_ref, wg_ref, wu_ref, wc_ref, bc_ref, wd_ref,
                y_ref, gt_ref, xn_ref, car_ref, *, tm, tiles_per_seq, short_seq):
    m = pl.program_id(0)
    j = pl.program_id(1)

    @pl.when(j == 0)
    def _():
        x = x_ref[...]
        xn_ref[...] = _rms(x, g_ref[...]).astype(BF16)
        y_ref[...] = x

    if not short_seq:
        @pl.when(m % tiles_per_seq == 0)
        def _():
            car_ref[j] = init_ref[0]

    xn = xn_ref[...]
    g = _dot(xn, wg_ref[...])
    u = _dot(xn, wu_ref[...])
    g1 = pltpu.roll(g, 1, 0)
    g2 = pltpu.roll(g, 2, 0)
    if short_seq:
        hist = init_ref[...]
        r8 = lax.broadcasted_iota(jnp.int32, g.shape, 0) % SUBLANES
        p1 = jnp.where(r8 == 0, pltpu.roll(hist, tm - 1, 0), g1)
        p2 = jnp.where(r8 < 2, hist, g2)
        gt_ref[...] = g
    else:
        prev = car_ref[j]
        r8 = lax.broadcasted_iota(jnp.int32, prev.shape, 0)
        p1 = jnp.concatenate([jnp.where(r8 < 1, pltpu.roll(prev, 1, 0), g1[0:SUBLANES]), g1[SUBLANES:]], axis=0)
        p2 = jnp.concatenate([jnp.where(r8 < 2, pltpu.roll(prev, 2, 0), g2[0:SUBLANES]), g2[SUBLANES:]], axis=0)
        tail = g[tm - SUBLANES:tm]
        car_ref[j] = tail
        gt_ref[0] = tail
    wc = wc_ref[...]
    gc = bc_ref[...] + wc[2:3] * g + wc[1:2] * p1 + wc[0:1] * p2
    hid = (jax.nn.gelu(gc) * u).astype(BF16)
    y_ref[...] += _dot(hid, wd_ref[...])


def _ffn(x, g, init, wg, wu, wc, bc, wd, layer, tm, tf, seq_len):
    m = x.shape[0]
    nff = D_FF // tf
    short_seq = seq_len < tm
    if short_seq:
        assert seq_len == SUBLANES and m == tm
        tiles_per_seq = 1
        init_spec = pl.BlockSpec((tm, tf), lambda i, j: (i, j))
        gt_spec = pl.BlockSpec((tm, tf), lambda i, j: (i, j))
        gt_shape = jax.ShapeDtypeStruct((m, D_FF), F32)
    else:
        assert seq_len % tm == 0
        tiles_per_seq = seq_len // tm
        init_spec = pl.BlockSpec((1, SUBLANES, tf), lambda i, j: (i // tiles_per_seq, 0, j))
        gt_spec = pl.BlockSpec((1, SUBLANES, tf), lambda i, j: (i, 0, j))
        gt_shape = jax.ShapeDtypeStruct((m // tm, SUBLANES, D_FF), F32)
    return pl.pallas_call(
        functools.partial(_ffn_kernel, tm=tm, tiles_per_seq=tiles_per_seq, short_seq=short_seq),
        grid=(m // tm, nff),
        in_specs=[pl.BlockSpec((tm, D_MODEL), lambda i, j: (i, 0)),
                  pl.BlockSpec((1, D_MODEL), lambda i, j: (0, 0)),
                  init_spec,
                  pl.BlockSpec((None, D_MODEL, tf), lambda i, j: (layer, 0, j)),
                  pl.BlockSpec((None, D_MODEL, tf), lambda i, j: (layer, 0, j)),
                  pl.BlockSpec((None, CONV_F, tf), lambda i, j: (layer, 0, j)),
                  pl.BlockSpec((None, 1, tf), lambda i, j: (layer, 0, j)),
                  pl.BlockSpec((None, tf, D_MODEL), lambda i, j: (layer, j, 0))],
        out_specs=[pl.BlockSpec((tm, D_MODEL), lambda i, j: (i, 0)), gt_spec],
        out_shape=[jax.ShapeDtypeStruct((m, D_MODEL), F32), gt_shape],
        scratch_shapes=[pltpu.VMEM((tm, D_MODEL), BF16), pltpu.VMEM((nff, SUBLANES, tf), F32)],
        compiler_params=_cparams(("arbitrary", "arbitrary")),
        name="conv_ffn",
    )(x, g, init, wg, wu, wc, bc, wd)


def _block_diag(w):
    n, c, d = w.shape
    eye = jnp.eye(n, dtype=w.dtype)
    return (eye[:, None, :, None] * w[:, :, None, :]).reshape(n * c, n * d)


def _suffix_tri(n):
    j = np.arange(n)[:, None]
    s = np.arange(n)[None, :]
    return jnp.asarray((j >= s).astype(np.float32), dtype=BF16)


def kernel(x_prompt, x_sample, cache_k, cache_v, state_lru_h, state_lru_conv, state_ffn_conv, page_table,
           g_mix, g_ffn, w_in_even, g_q, g_k, sb_bias, w_conv_lru, b_conv_lru, w_rgate, b_rgate, w_igate,
           b_igate, lru_lambda, w_out_even, w_in_odd, g_v, w_spatial, b_spatial, w_out_odd,
           w_gate, w_up, w_ffn_conv, b_ffn_conv, w_down):
    bp, tp, _ = x_prompt.shape
    bs, ts, _ = x_sample.shape
    depth = g_mix.shape[0]
    n_pool = cache_k.shape[1]
    mp, ms = bp * tp, bs * ts
    assert ts == SUBLANES and tp % 512 == 0

    tm_p, tm_s = 512, ms
    tm_f = 1024
    tk, nsub, unroll, chunk = 256, 4, 4, 1024
    tt_p = 256
    tf = 1024
    n_group = 32

    yp = x_prompt.reshape(mp, D_MODEL)
    ys = x_sample.reshape(ms, D_MODEL)

    head_mean = jnp.asarray(np.kron(np.eye(H_A), np.full((HD_A, HD_A), 1.0 / HD_A)), dtype=BF16)
    tri = _suffix_tri(tk)
    tri_s = _suffix_tri(2 * PAGE)
    ck = jnp.transpose(cache_k, (0, 1, 3, 4, 2)).reshape(cache_k.shape[0], n_pool, W_A, PAGE)
    cv = jnp.transpose(cache_v, (0, 1, 3, 4, 2)).reshape(cache_v.shape[0], n_pool, W_A, PAGE)
    row_head = np.arange(H_A * SUBLANES)[:, None] // SUBLANES
    lane_head = np.arange(W_A)[None, :] // HD_A
    qbd_mask = jnp.asarray(row_head == lane_head)

    ffn_w = (w_gate.astype(BF16), w_up.astype(BF16), w_ffn_conv, b_ffn_conv.reshape(depth, 1, D_FF),
             w_down.astype(BF16))
    outs = {n: [] for n in ("kp", "vp", "ks", "vs", "hp", "hs", "cp", "cs", "chp", "chs", "fp", "fs")}
    last_chunk_start = ((tp - 1) // CHUNK) * CHUNK

    for l in range(depth):
        gm = g_mix[l].reshape(1, D_MODEL)
        if l % 2 == 0:
            e = l // 2
            w_in = w_in_even[e].astype(BF16)
            gq = (jnp.tile(g_q[e], H_A) * (HD_A ** -0.5 * LOG2E)).reshape(1, W_A)
            gk = jnp.tile(g_k[e], H_A).reshape(1, W_A)
            bias2 = sb_bias[e] * LOG2E
            bias_hi = bias2.astype(BF16).astype(F32)
            bias_hl = jnp.stack([bias_hi, (bias2 - bias_hi).astype(BF16).astype(F32)])
            wr = _block_diag(w_rgate[e]).astype(BF16)
            wi = _block_diag(w_igate[e]).astype(BF16)
            br = b_rgate[e].reshape(1, W_B)
            bi = b_igate[e].reshape(1, W_B)
            lam = lru_lambda[e].reshape(1, W_B)
            bc = b_conv_lru[e].reshape(1, W_B)
            w_out = w_out_even[e].astype(BF16)
            lru_w = (w_conv_lru[e], bc, wr, br, wi, bi, lam)

            q, kt, vt, kb, vb, xb, gb = _even_in(yp, gm, w_in, head_mean, gq, gk, tm_p, tp)
            oa = _attn_prompt(q.reshape(bp, tp, W_A), kb.reshape(bp, tp, W_A), vb.reshape(bp, tp, W_A),
                              bias_hl, tri, tk, nsub, unroll, chunk)
            xb3 = xb.reshape(bp, tp, W_B)
            y3, hl = _lru_out(xb3, gb.reshape(bp, tp, W_B), jnp.zeros((bp, 1, W_B), F32),
                              jnp.zeros((bp, SUBLANES, W_B), F32), yp.reshape(bp, tp, D_MODEL), oa,
                              *lru_w, w_out, tt_p, False)
            yp = y3.reshape(mp, D_MODEL)
            to_thd = lambda a: jnp.transpose(a.reshape(bp, H_A, HD_A, tp), (0, 3, 1, 2))
            outs["kp"].append(to_thd(kt))
            outs["vp"].append(to_thd(vt))
            outs["hp"].append(hl[:, SUBLANES - 1])
            outs["cp"].append(xb3[:, tp - (CONV_B - 1):])

            q, k, v, kb, vb, xb, gb = _even_in(ys, gm, w_in, head_mean, gq, gk, tm_s, None)
            qbd = jnp.where(qbd_mask, jnp.tile(q.reshape(bs, ts, W_A), (1, H_A, 1)), jnp.zeros((), BF16))
            brow = jnp.repeat(bias2, SUBLANES).reshape(H_A * SUBLANES, 1)
            pad = ((0, 0), (0, 0), (0, PAGE - ts))
            knew = jnp.pad(jnp.swapaxes(kb.reshape(bs, ts, W_A), 1, 2), pad)
            vnew = jnp.pad(jnp.swapaxes(vb.reshape(bs, ts, W_A), 1, 2), pad)
            oa = _attn_sample(page_table, qbd, brow, knew, vnew, tri_s, ck, cv, e, n_group)
            xb3 = xb.reshape(bs, ts, W_B)
            hist = jnp.pad(state_lru_conv[e], ((0, 0), (0, SUBLANES - (CONV_B - 1)), (0, 0))).reshape(1, ms, W_B)
            h0 = jnp.repeat(state_lru_h[e], ts, axis=0).reshape(1, ms, W_B)
            y3, hl = _lru_out(xb.reshape(1, ms, W_B), gb.reshape(1, ms, W_B), h0, hist,
                              ys.reshape(1, ms, D_MODEL), oa.reshape(1, ms, W_A), *lru_w, w_out, ms, True)
            ys = y3.reshape(ms, D_MODEL)
            outs["ks"].append(k.reshape(bs, ts, H_A, HD_A))
            outs["vs"].append(v.reshape(bs, ts, H_A, HD_A))
            outs["hs"].append(hl.reshape(bs, ts, W_B)[:, ts - 1])
            outs["cs"].append(xb3[:, ts - (CONV_B - 1):])
        else:
            o = l // 2
            w_in = w_in_odd[o].astype(BF16)
            gv = g_v[o].reshape(1, W_C)
            w_out = w_out_odd[o].astype(BF16)
            w_tril = jnp.tril(w_spatial[o])
            wmix_p = w_tril.astype(BF16)
            bmix_p = jnp.repeat(b_spatial[o].T, CW_C, axis=1)
            taps = jnp.stack([jnp.pad(jnp.diagonal(w_tril[:, :ts, :ts], offset=-k, axis1=1, axis2=2),
                                      ((0, 0), (k, 0))) for k in range(ts)])
            wmix_s = jnp.repeat(jnp.transpose(taps, (0, 2, 1)), CW_C, axis=2)
            bmix_s = bmix_p[:ts]

            yp, vn = _odd_mixer(yp, gm, w_in, gv, wmix_p, bmix_p, w_out, tm_p, CHUNK)
            outs["chp"].append(vn.reshape(bp, tp, W_C)[:, last_chunk_start:])
            ys, vn = _odd_mixer(ys, gm, w_in, gv, wmix_s, bmix_s, w_out, tm_s, ts)
            outs["chs"].append(vn.reshape(bs, ts, W_C))

        gf = g_ffn[l].reshape(1, D_MODEL)
        yp, gt = _ffn(yp, gf, jnp.zeros((bp, SUBLANES, D_FF), F32), *ffn_w, l, tm_f, tf, tp)
        tiles = tp // tm_f
        outs["fp"].append(gt.reshape(bp, tiles, SUBLANES, D_FF)[:, tiles - 1, SUBLANES - (CONV_F - 1):])
        hist = jnp.pad(state_ffn_conv[l], ((0, 0), (0, SUBLANES - (CONV_F - 1)), (0, 0))).reshape(ms, D_FF)
        ys, gt = _ffn(ys, gf, hist, *ffn_w, l, tm_s, tf, ts)
        outs["fs"].append(gt.reshape(bs, ts, D_FF)[:, ts - (CONV_F - 1):])

    st = lambda n: jnp.stack(outs[n])
    return (yp.reshape(bp, tp, D_MODEL), ys.reshape(bs, ts, D_MODEL),
            st("kp"), st("vp"), st("ks"), st("vs"), st("hp"), st("hs"), st("cp"), st("cs"),
            st("chp"), st("chs"), st("fp"), st("fs"))
```

```python
import functools
import math

import numpy as np
import jax
import jax.numpy as jnp
from jax import lax
from jax.experimental import pallas as pl
from jax.experimental.pallas import tpu as pltpu

F32 = jnp.float32
BF16 = jnp.bfloat16

D_MODEL = 1024
H_A, HD_A = 8, 64
W_A = H_A * HD_A
H_B, BW_B = 8, 64
W_B = H_B * BW_B
CONV_B = 4
C_LRU = 8.0
H_C, CW_C = 8, 128
W_C = H_C * CW_C
CHUNK = 128
D_FF = 3 * D_MODEL
CONV_F = 3
PAGE = 128
EPS = 1e-6
D_IN_EVEN = 3 * W_A + 2 * W_B
LOG2E = math.log2(math.e)

LANES = 128
SUBLANES = 8
VMEM_LIMIT = 56 * 1024 * 1024


def _cparams(sem):
    return pltpu.CompilerParams(dimension_semantics=sem, vmem_limit_bytes=VMEM_LIMIT)


def _rms(x, g):
    ms = jnp.mean(x * x, axis=-1, keepdims=True)
    return x * lax.rsqrt(ms + EPS) * g


def _dot(a, b):
    return jnp.dot(a, b, preferred_element_type=F32)


def _dot_nt(a, b):
    return lax.dot_general(a, b, (((1,), (1,)), ((), ())), preferred_element_type=F32)


def _even_in_kernel(x_ref, g_ref, w_ref, pm_ref, gq_ref, gk_ref,
                    q_ref, k_ref, v_ref, kb_ref, vb_ref, xb_ref, gb_ref, *, kv_transposed):
    xn = _rms(x_ref[...], g_ref[...]).astype(BF16)
    p = _dot(xn, w_ref[...])
    pm = pm_ref[...]

    def head_norm(t, g):
        sq = t * t
        hi = sq.astype(BF16)
        lo = (sq - hi.astype(F32)).astype(BF16)
        ms = _dot(hi, pm) + _dot(lo, pm)
        return t * lax.rsqrt(ms + EPS) * g

    q = head_norm(p[:, 0:W_A], gq_ref[...])
    k = head_norm(p[:, W_A:2 * W_A], gk_ref[...])
    v = p[:, 2 * W_A:3 * W_A]
    q_ref[...] = q.astype(BF16)
    if kv_transposed:
        k_ref[0] = k.T
        v_ref[0] = v.T
    else:
        k_ref[...] = k
        v_ref[...] = v
    kb_ref[...] = k.astype(BF16)
    vb_ref[...] = v.astype(BF16)
    xb_ref[...] = p[:, 3 * W_A:3 * W_A + W_B]
    gb_ref[...] = p[:, 3 * W_A + W_B:]


def _even_in(x, g, w, pm, gq, gk, tm, seq_len):
    m = x.shape[0]
    row = lambda i: (i, 0)
    fixed = lambda i: (0, 0)
    f32o = jax.ShapeDtypeStruct((m, W_A), F32)
    bfo = jax.ShapeDtypeStruct((m, W_A), BF16)
    blk = pl.BlockSpec((tm, W_A), row)
    kv_shape, kv_blk = f32o, blk
    if seq_len is not None:
        tps = seq_len // tm
        kv_shape = jax.ShapeDtypeStruct((m // seq_len, W_A, seq_len), F32)
        kv_blk = pl.BlockSpec((1, W_A, tm), lambda i: (i // tps, 0, i % tps))
    return pl.pallas_call(
        functools.partial(_even_in_kernel, kv_transposed=seq_len is not None),
        grid=(m // tm,),
        in_specs=[pl.BlockSpec((tm, D_MODEL), row), pl.BlockSpec((1, D_MODEL), fixed),
                  pl.BlockSpec((D_MODEL, D_IN_EVEN), fixed), pl.BlockSpec((W_A, W_A), fixed),
                  pl.BlockSpec((1, W_A), fixed), pl.BlockSpec((1, W_A), fixed)],
        out_specs=[blk, kv_blk, kv_blk, blk, blk, blk, blk],
        out_shape=[bfo, kv_shape, kv_shape, bfo, bfo, f32o, f32o],
        compiler_params=_cparams(("parallel",)),
        name="even_in",
    )(x, g, w, pm, gq, gk)


def _softplus2(z):
    return jnp.maximum(z, jnp.log2(1.0 + jnp.exp2(jnp.minimum(z, 100.0))))


def _sb_weights(z, tri, carry, mask):
    sp = _softplus2(z)
    if mask is not None:
        sp = jnp.where(mask, sp, 0.0)
    c = _dot(sp.astype(BF16), tri)
    w = jnp.exp2(z - c - carry)
    if mask is not None:
        w = jnp.where(mask, w, 0.0)
    return w.astype(BF16), carry + c[:, 0:1]


def _attn_prompt_kernel(bias_ref, q_ref, k_ref, v_ref, tri_ref, o_ref, ka_ref, *, tk, nsub, unroll, chunk):
    pair = pl.program_id(1)
    i = pl.program_id(2)
    t = k_ref.shape[1]
    rows = 2 * nsub * tk

    @pl.when(i == 0)
    def _():
        ka_ref[:, 0:LANES] = k_ref[0]
        lane_t = lax.broadcasted_iota(jnp.int32, (t, LANES), 1)
        ka_ref[:, LANES:2 * LANES] = jnp.where(lane_t < 2, 1.0, 0.0).astype(BF16)

    q = q_ref[0]
    tri = tri_ref[...]
    lane = lax.broadcasted_iota(jnp.int32, (tk, LANES), 1)
    first = lane < HD_A
    zero = jnp.zeros((tk, LANES), BF16)
    ext = []
    for h in range(2):
        hi = bias_ref[0, 2 * pair + h]
        lo = bias_ref[1, 2 * pair + h]
        ext.append(jnp.where(lane == 0, hi, jnp.where(lane == 1, lo, 0.0)).astype(BF16))
    chains = []
    for s in range(nsub):
        qs = q[s * tk:(s + 1) * tk]
        chains.append(jnp.concatenate([jnp.where(first, qs, zero), ext[0]], axis=1))
        chains.append(jnp.concatenate([jnp.where(first, zero, qs), ext[1]], axis=1))
    qa = jnp.concatenate(chains, axis=0)

    def sweep(j_hi, nblk, qa_r, carry, acc, diag_rows):
        r = qa_r.shape[0]
        ch = min(chunk, r)
        for u in range(nblk):
            lo = pl.multiple_of((j_hi - u) * tk, tk)
            kj = ka_ref[pl.ds(lo, tk), :]
            vj = v_ref[0, pl.ds(lo, tk), :]
            cs, accs = [], []
            for c0 in range(0, r, ch):
                sl = slice(c0, c0 + ch)
                z = _dot_nt(qa_r[sl], kj)
                mask = None
                if diag_rows and u == 0 and c0 < diag_rows:
                    rr = lax.broadcasted_iota(jnp.int32, (ch, tk), 0) + c0
                    cc = lax.broadcasted_iota(jnp.int32, (ch, tk), 1)
                    mask = (rr >= diag_rows) | (cc < (rr % tk))
                w, c2 = _sb_weights(z, tri, carry[sl], mask)
                cs.append(c2)
                accs.append(acc[sl] + _dot(w, vj))
            carry = cs[0] if len(cs) == 1 else jnp.concatenate(cs, axis=0)
            acc = accs[0] if len(accs) == 1 else jnp.concatenate(accs, axis=0)
        return carry, acc

    carry = jnp.zeros((rows, 1), F32)
    acc = jnp.zeros((rows, LANES), F32)
    for m in reversed(range(nsub)):
        r0 = m * 2 * tk
        c_m, a_m = sweep(nsub * i + m, 1, qa[r0:], carry[r0:], acc[r0:], 2 * tk)
        carry = jnp.concatenate([carry[:r0], c_m], axis=0) if r0 else c_m
        acc = jnp.concatenate([acc[:r0], a_m], axis=0) if r0 else a_m

    nloop = nsub * i
    carry, acc = lax.fori_loop(
        0, nloop // unroll,
        lambda n, st: sweep(nloop - 1 - unroll * n, unroll, qa, st[0], st[1], 0), (carry, acc))
    for s in range(nsub):
        a0 = acc[(2 * s) * tk:(2 * s + 1) * tk]
        a1 = acc[(2 * s + 1) * tk:(2 * s + 2) * tk]
        o_ref[0, s * tk:(s + 1) * tk, :] = jnp.where(first, a0, a1).astype(o_ref.dtype)


def _attn_prompt(q, k, v, bias_hl, tri, tk, nsub, unroll, chunk):
    b, t, _ = q.shape
    npair = W_A // LANES
    tq = tk * nsub
    assert t % tq == 0 and nsub % unroll == 0
    return pl.pallas_call(
        functools.partial(_attn_prompt_kernel, tk=tk, nsub=nsub, unroll=unroll, chunk=chunk),
        grid=(b, npair, t // tq),
        in_specs=[pl.BlockSpec(memory_space=pltpu.SMEM),
                  pl.BlockSpec((1, tq, LANES), lambda bi, p, i: (bi, i, p)),
                  pl.BlockSpec((1, t, LANES), lambda bi, p, i: (bi, 0, p)),
                  pl.BlockSpec((1, t, LANES), lambda bi, p, i: (bi, 0, p)),
                  pl.BlockSpec((tk, tk), lambda bi, p, i: (0, 0))],
        out_specs=pl.BlockSpec((1, tq, LANES), lambda bi, p, i: (bi, i, p)),
        out_shape=jax.ShapeDtypeStruct((b, t, W_A), BF16),
        scratch_shapes=[pltpu.VMEM((t, 2 * LANES), BF16)],
        compiler_params=_cparams(("parallel", "parallel", "arbitrary")),
        name="attn_prompt",
    )(bias_hl, q, k, v, tri)


def _attn_sample_kernel(pt_ref, qbd_ref, brow_ref, kn_ref, vn_ref, tri_ref, *rest, n_group):
    del pt_ref
    pages = rest[:2 * n_group]
    o_ref, acc_ref, car_ref = rest[2 * n_group:]
    p = pl.program_id(1)
    qbd = qbd_ref[0]
    brow = brow_ref[...]
    tri = tri_ref[...]
    rows = H_A * SUBLANES
    blk = 2 * PAGE
    nblk = n_group * PAGE // blk

    @pl.when(p == 0)
    def _():
        z = _dot(qbd, kn_ref[0]) + brow
        rr = lax.broadcasted_iota(jnp.int32, (rows, PAGE), 0) % SUBLANES
        cc = lax.broadcasted_iota(jnp.int32, (rows, PAGE), 1)
        w, carry = _sb_weights(z, tri[0:PAGE, 0:PAGE], jnp.zeros((rows, 1), F32), cc < rr)
        acc_ref[...] = _dot_nt(w, vn_ref[0])
        car_ref[...] = carry

    kall = jnp.concatenate([pages[g][0, 0] for g in range(n_group)], axis=1).astype(BF16)
    vall = jnp.concatenate([pages[n_group + g][0, 0] for g in range(n_group)], axis=1).astype(BF16)
    z = _dot(qbd, kall) + brow
    spb = _softplus2(z).astype(BF16)
    cs = [_dot(spb[:, b * blk:(b + 1) * blk], tri) for b in range(nblk)]
    run = car_ref[...]
    args = [None] * nblk
    for b in reversed(range(nblk)):
        args[b] = z[:, b * blk:(b + 1) * blk] - cs[b] - run
        run = run + cs[b][:, 0:1]
    w = jnp.exp2(jnp.concatenate(args, axis=1)).astype(BF16)
    car_ref[...] = run
    acc_ref[...] += _dot_nt(w, vall)

    @pl.when(p == pl.num_programs(1) - 1)
    def _():
        acc = acc_ref[...]
        lane_head = lax.broadcasted_iota(jnp.int32, (SUBLANES, W_A), 1) // HD_A
        out = jnp.zeros((SUBLANES, W_A), F32)
        for h in range(H_A):
            out = jnp.where(lane_head == h, acc[h * SUBLANES:(h + 1) * SUBLANES, :], out)
        o_ref[0] = out.astype(o_ref.dtype)


def _attn_sample(page_table, qbd, brow, knew, vnew, tri, cache_k, cache_v, e, n_group):
    bs, n_pages = page_table.shape
    steps = n_pages // n_group
    rows = H_A * SUBLANES

    def page_spec(i):
        return pl.BlockSpec(
            (1, 1, W_A, PAGE),
            lambda b, p, pt, i=i: (e, pt[b, (steps - 1 - p) * n_group + i], 0, 0))

    grid_spec = pltpu.PrefetchScalarGridSpec(
        num_scalar_prefetch=1,
        grid=(bs, steps),
        in_specs=[pl.BlockSpec((1, rows, W_A), lambda b, p, pt: (b, 0, 0)),
                  pl.BlockSpec((rows, 1), lambda b, p, pt: (0, 0)),
                  pl.BlockSpec((1, W_A, PAGE), lambda b, p, pt: (b, 0, 0)),
                  pl.BlockSpec((1, W_A, PAGE), lambda b, p, pt: (b, 0, 0)),
                  pl.BlockSpec((2 * PAGE, 2 * PAGE), lambda b, p, pt: (0, 0))]
                 + [page_spec(i) for i in range(n_group)] * 2,
        out_specs=pl.BlockSpec((1, SUBLANES, W_A), lambda b, p, pt: (b, 0, 0)),
        scratch_shapes=[pltpu.VMEM((rows, W_A), F32), pltpu.VMEM((rows, 1), F32)],
    )
    return pl.pallas_call(
        functools.partial(_attn_sample_kernel, n_group=n_group),
        grid_spec=grid_spec,
        out_shape=jax.ShapeDtypeStruct((bs, SUBLANES, W_A), BF16),
        compiler_params=_cparams(("parallel", "arbitrary")),
        name="attn_sample",
    )(page_table, qbd, brow, knew, vnew, tri, *([cache_k] * n_group), *([cache_v] * n_group))


def _lru_kernel(xb_ref, gb_ref, h0_ref, cb_ref, res_ref, oa_ref, wc_ref, bc_ref, wr_ref, br_ref, wi_ref, bi_ref,
                lam_ref, wo_ref, y_ref, hl_ref, xp_ref, hc_ref, *, tt, short_seq):
    t = pl.program_id(1)

    if not short_seq:
        @pl.when(t == 0)
        def _():
            xp_ref[...] = cb_ref[0]
            hc_ref[...] = h0_ref[0]

    x = xb_ref[0]
    row = lax.broadcasted_iota(jnp.int32, (tt, W_B), 0)
    r8 = row % SUBLANES
    wc = wc_ref[...]
    xc = bc_ref[...] + wc[3:4] * x
    if short_seq:
        hist = cb_ref[0]
        for k in range(1, CONV_B):
            shift = (tt + k - (CONV_B - 1)) % tt
            hist_k = pltpu.roll(hist, shift, 0) if shift else hist
            xc = xc + wc[CONV_B - 1 - k:CONV_B - k] * jnp.where(r8 < k, hist_k, pltpu.roll(x, k, 0))
    else:
        prev = xp_ref[...]
        top = lax.broadcasted_iota(jnp.int32, (SUBLANES, W_B), 0)
        for k in range(1, CONV_B):
            xk = pltpu.roll(x, k, 0)
            prev_k = jnp.concatenate([jnp.where(top < k, pltpu.roll(prev, k, 0), xk[0:SUBLANES]), xk[SUBLANES:]],
                                     axis=0)
            xc = xc + wc[CONV_B - 1 - k:CONV_B - k] * prev_k
        xp_ref[...] = x[tt - SUBLANES:tt]

    xcb = xc.astype(BF16)
    r = jax.nn.sigmoid(_dot(xcb, wr_ref[...]) + br_ref[...])
    ig = jax.nn.sigmoid(_dot(xcb, wi_ref[...]) + bi_ref[...])
    nl = -lam_ref[...]
    sp_lam = jnp.maximum(nl, 0.0) + jnp.log1p(jnp.exp(-jnp.abs(nl)))
    log_a = -C_LRU * r * sp_lam
    a = jnp.exp(log_a)
    u = jnp.sqrt((1.0 + a * a) * jnp.tanh(-log_a)) * (ig * xc)

    grp = (tt // SUBLANES, SUBLANES, W_B)
    a, u = a.reshape(grp), u.reshape(grp)
    r8g = lax.broadcasted_iota(jnp.int32, grp, 1)
    for d in (1, 2, 4):
        keep = r8g >= d
        a_sh = jnp.where(keep, pltpu.roll(a, d, 1), 1.0)
        u_sh = jnp.where(keep, pltpu.roll(u, d, 1), 0.0)
        u = a * u_sh + u
        a = a * a_sh
    a, u = a.reshape(tt, W_B), u.reshape(tt, W_B)
    if short_seq:
        h = a * h0_ref[0] + u
        hl_ref[0] = h
    else:
        h_in = hc_ref[...]
        parts = []
        for g in range(tt // SUBLANES):
            sl = slice(g * SUBLANES, (g + 1) * SUBLANES)
            h_g = a[sl] * h_in + u[sl]
            h_in = h_g[SUBLANES - 1:SUBLANES]
            parts.append(h_g)
        h = jnp.concatenate(parts, axis=0)
        hc_ref[...] = h_in
        hl_ref[0] = parts[-1]
    ob = (h * jax.nn.gelu(gb_ref[0])).astype(BF16)
    y_ref[0] = (res_ref[0] + _dot(oa_ref[0], wo_ref[0:W_A, :]) + _dot(ob, wo_ref[W_A:W_A + W_B, :]))


def _lru_out(xb, gb, h0, cbuf, res, oa, wc, bc, wr, br, wi, bi, lam, wo, tt, short_seq):
    b, t, _ = xb.shape
    seq = lambda bi, ti: (bi, ti, 0)
    per_b = lambda bi, ti: (bi, 0, 0)
    fixed = lambda bi, ti: (0, 0)
    if short_seq:
        assert t == tt
        h0_spec, cb_spec = pl.BlockSpec((1, tt, W_B), seq), pl.BlockSpec((1, tt, W_B), seq)
        hl_spec, hl_shape = pl.BlockSpec((1, tt, W_B), seq), jax.ShapeDtypeStruct((b, t, W_B), F32)
    else:
        h0_spec, cb_spec = pl.BlockSpec((1, 1, W_B), per_b), pl.BlockSpec((1, SUBLANES, W_B), per_b)
        hl_spec, hl_shape = pl.BlockSpec((1, SUBLANES, W_B), per_b), jax.ShapeDtypeStruct((b, SUBLANES, W_B), F32)
    return pl.pallas_call(
        functools.partial(_lru_kernel, tt=tt, short_seq=short_seq),
        grid=(b, t // tt),
        in_specs=[pl.BlockSpec((1, tt, W_B), seq), pl.BlockSpec((1, tt, W_B), seq), h0_spec, cb_spec,
                  pl.BlockSpec((1, tt, D_MODEL), seq), pl.BlockSpec((1, tt, W_A), seq),
                  pl.BlockSpec((CONV_B, W_B), fixed), pl.BlockSpec((1, W_B), fixed),
                  pl.BlockSpec((W_B, W_B), fixed), pl.BlockSpec((1, W_B), fixed),
                  pl.BlockSpec((W_B, W_B), fixed), pl.BlockSpec((1, W_B), fixed),
                  pl.BlockSpec((1, W_B), fixed), pl.BlockSpec((W_A + W_B, D_MODEL), fixed)],
        out_specs=[pl.BlockSpec((1, tt, D_MODEL), seq), hl_spec],
        out_shape=[jax.ShapeDtypeStruct((b, t, D_MODEL), F32), hl_shape],
        scratch_shapes=[pltpu.VMEM((SUBLANES, W_B), F32), pltpu.VMEM((1, W_B), F32)],
        compiler_params=_cparams(("parallel", "arbitrary")),
        name="rglru_out",
    )(xb, gb, h0, cbuf, res, oa, wc, bc, wr, br, wi, bi, lam, wo)


def _odd_kernel(x_ref, g_ref, win_ref, gv_ref, wmix_ref, bmix_ref, wout_ref, y_ref, vn_ref, um_ref,
                *, tm, rblk):
    x = x_ref[...]
    xn = _rms(x, g_ref[...]).astype(BF16)
    z = jax.nn.gelu(_dot(xn, win_ref[...]))
    for h in range(H_C):
        lo, hi = h * CW_C, (h + 1) * CW_C
        vh = z[:, W_C + lo:W_C + hi]
        ms = jnp.mean(vh * vh, axis=-1, keepdims=True)
        vnh = vh * lax.rsqrt(ms + EPS) * gv_ref[:, lo:hi]
        vn_ref[:, lo:hi] = vnh
        if rblk == SUBLANES:
            v3 = vnh.reshape(tm // SUBLANES, SUBLANES, CW_C)
            mixed = wmix_ref[0, :, lo:hi][None] * v3 + bmix_ref[:, lo:hi][None]
            for kk in range(1, SUBLANES):
                mixed = mixed + wmix_ref[kk, :, lo:hi][None] * pltpu.roll(v3, kk, 1)
            um_ref[:, lo:hi] = (z[:, lo:hi] * mixed.reshape(tm, CW_C)).astype(BF16)
        else:
            vnb = vnh.astype(BF16)
            wm = wmix_ref[h]
            for c in range(tm // rblk):
                r0, r1 = c * rblk, (c + 1) * rblk
                mixed = _dot(wm, vnb[r0:r1]) + bmix_ref[:, lo:hi]
                um_ref[r0:r1, lo:hi] = (z[r0:r1, lo:hi] * mixed).astype(BF16)
    y_ref[...] = x + _dot(um_ref[...], wout_ref[...])


def _odd_mixer(x, g, win, gv, wmix, bmix, wout, tm, rblk):
    m = x.shape[0]
    row = lambda i: (i, 0)
    fixed = lambda i: (0, 0)
    return pl.pallas_call(
        functools.partial(_odd_kernel, tm=tm, rblk=rblk),
        grid=(m // tm,),
        in_specs=[pl.BlockSpec((tm, D_MODEL), row), pl.BlockSpec((1, D_MODEL), fixed),
                  pl.BlockSpec((D_MODEL, 2 * W_C), fixed), pl.BlockSpec((1, W_C), fixed),
                  pl.BlockSpec(wmix.shape, lambda i: (0, 0, 0)),
                  pl.BlockSpec((rblk, W_C), fixed), pl.BlockSpec((W_C, D_MODEL), fixed)],
        out_specs=[pl.BlockSpec((tm, D_MODEL), row), pl.BlockSpec((tm, W_C), row)],
        out_shape=[jax.ShapeDtypeStruct((m, D_MODEL), F32), jax.ShapeDtypeStruct((m, W_C), F32)],
        scratch_shapes=[pltpu.VMEM((tm, W_C), BF16)],
        compiler_params=_cparams(("parallel",)),
        name="odd_mixer",
    )(x, g, win, gv, wmix, bmix, wout)


def _ffn_kernel(x_ref, g_ref, init_ref, wg_ref, wu_ref, wc_ref, bc_ref, wd_ref,
                y_ref, gt_ref, xn_ref, car_ref, *, tm, tiles_per_seq, short_seq):
    m = pl.program_id(0)
    j = pl.program_id(1)

    @pl.when(j == 0)
    def _():
        x = x_ref[...]
        xn_ref[...] = _rms(x, g_ref[...]).astype(BF16)
        y_ref[...] = x

    if not short_seq:
        @pl.when(m % tiles_per_seq == 0)
        def _():
            car_ref[j] = init_ref[0]

    xn = xn_ref[...]
    g = _dot(xn, wg_ref[...])
    u = _dot(xn, wu_ref[...])
    g1 = pltpu.roll(g, 1, 0)
    g2 = pltpu.roll(g, 2, 0)
    if short_seq:
        hist = init_ref[...]
        r8 = lax.broadcasted_iota(jnp.int32, g.shape, 0) % SUBLANES
        p1 = jnp.where(r8 == 0, pltpu.roll(hist, tm - 1, 0), g1)
        p2 = jnp.where(r8 < 2, hist, g2)
        gt_ref[...] = g
    else:
        prev = car_ref[j]
        r8 = lax.broadcasted_iota(jnp.int32, prev.shape, 0)
        p1 = jnp.concatenate([jnp.where(r8 < 1, pltpu.roll(prev, 1, 0), g1[0:SUBLANES]), g1[SUBLANES:]], axis=0)
        p2 = jnp.concatenate([jnp.where(r8 < 2, pltpu.roll(prev, 2, 0), g2[0:SUBLANES]), g2[SUBLANES:]], axis=0)
        tail = g[tm - SUBLANES:tm]
        car_ref[j] = tail
        gt_ref[0] = tail
    wc = wc_ref[...]
    gc = bc_ref[...] + wc[2:3] * g + wc[1:2] * p1 + wc[0:1] * p2
    hid = (jax.nn.gelu(gc) * u).astype(BF16)
    y_ref[...] += _dot(hid, wd_ref[...])


def _ffn(x, g, init, wg, wu, wc, bc, wd, layer, tm, tf, seq_len):
    m = x.shape[0]
    nff = D_FF // tf
    short_seq = seq_len < tm
    if short_seq:
        assert seq_len == SUBLANES and m == tm
        tiles_per_seq = 1
        init_spec = pl.BlockSpec((tm, tf), lambda i, j: (i, j))
        gt_spec = pl.BlockSpec((tm, tf), lambda i, j: (i, j))
        gt_shape = jax.ShapeDtypeStruct((m, D_FF), F32)
    else:
        assert seq_len % tm == 0
        tiles_per_seq = seq_len // tm
        init_spec = pl.BlockSpec((1, SUBLANES, tf), lambda i, j: (i // tiles_per_seq, 0, j))
        gt_spec = pl.BlockSpec((1, SUBLANES, tf), lambda i, j: (i, 0, j))
        gt_shape = jax.ShapeDtypeStruct((m // tm, SUBLANES, D_FF), F32)
    return pl.pallas_call(
        functools.partial(_ffn_kernel, tm=tm, tiles_per_seq=tiles_per_seq, short_seq=short_seq),
        grid=(m // tm, nff),
        in_specs=[pl.BlockSpec((tm, D_MODEL), lambda i, j: (i, 0)),
                  pl.BlockSpec((1, D_MODEL), lambda i, j: (0, 0)),
                  init_spec,
                  pl.BlockSpec((None, D_MODEL, tf), lambda i, j: (layer, 0, j)),
                  pl.BlockSpec((None, D_MODEL, tf), lambda i, j: (layer, 0, j)),
                  pl.BlockSpec((None, CONV_F, tf), lambda i, j: (layer, 0, j)),
                  pl.BlockSpec((None, 1, tf), lambda i, j: (layer, 0, j)),
                  pl.BlockSpec((None, tf, D_MODEL), lambda i, j: (layer, j, 0))],
        out_specs=[pl.BlockSpec((tm, D_MODEL), lambda i, j: (i, 0)), gt_spec],
        out_shape=[jax.ShapeDtypeStruct((m, D_MODEL), F32), gt_shape],
        scratch_shapes=[pltpu.VMEM((tm, D_MODEL), BF16), pltpu.VMEM((nff, SUBLANES, tf), F32)],
        compiler_params=_cparams(("arbitrary", "arbitrary")),
        name="conv_ffn",
    )(x, g, init, wg, wu, wc, bc, wd)


def _block_diag(w):
    n, c, d = w.shape
    eye = jnp.eye(n, dtype=w.dtype)
    return (eye[:, None, :, None] * w[:, :, None, :]).reshape(n * c, n * d)


def _suffix_tri(n):
    j = np.arange(n)[:, None]
    s = np.arange(n)[None, :]
    return jnp.asarray((j >= s).astype(np.float32), dtype=BF16)


def kernel(x_prompt, x_sample, cache_k, cache_v, state_lru_h, state_lru_conv, state_ffn_conv, page_table,
           g_mix, g_ffn, w_in_even, g_q, g_k, sb_bias, w_conv_lru, b_conv_lru, w_rgate, b_rgate, w_igate,
           b_igate, lru_lambda, w_out_even, w_in_odd, g_v, w_spatial, b_spatial, w_out_odd,
           w_gate, w_up, w_ffn_conv, b_ffn_conv, w_down):
    bp, tp, _ = x_prompt.shape
    bs, ts, _ = x_sample.shape
    depth = g_mix.shape[0]
    n_pool = cache_k.shape[1]
    mp, ms = bp * tp, bs * ts
    assert ts == SUBLANES and tp % 512 == 0

    tm_p, tm_s = 512, ms
    tm_f = 1024
    tk, nsub, unroll, chunk = 256, 8, 8, 1024
    tt_p = 256
    tf = 1024
    n_group = 32

    yp = x_prompt.reshape(mp, D_MODEL)
    ys = x_sample.reshape(ms, D_MODEL)

    head_mean = jnp.asarray(np.kron(np.eye(H_A), np.full((HD_A, HD_A), 1.0 / HD_A)), dtype=BF16)
    tri = _suffix_tri(tk)
    tri_s = _suffix_tri(2 * PAGE)
    ck = jnp.transpose(cache_k, (0, 1, 3, 4, 2)).reshape(cache_k.shape[0], n_pool, W_A, PAGE)
    cv = jnp.transpose(cache_v, (0, 1, 3, 4, 2)).reshape(cache_v.shape[0], n_pool, W_A, PAGE)
    row_head = np.arange(H_A * SUBLANES)[:, None] // SUBLANES
    lane_head = np.arange(W_A)[None, :] // HD_A
    qbd_mask = jnp.asarray(row_head == lane_head)

    ffn_w = (w_gate.astype(BF16), w_up.astype(BF16), w_ffn_conv, b_ffn_conv.reshape(depth, 1, D_FF),
             w_down.astype(BF16))
    outs = {n: [] for n in ("kp", "vp", "ks", "vs", "hp", "hs", "cp", "cs", "chp", "chs", "fp", "fs")}
    last_chunk_start = ((tp - 1) // CHUNK) * CHUNK

    for l in range(depth):
        gm = g_mix[l].reshape(1, D_MODEL)
        if l % 2 == 0:
            e = l // 2
            w_in = w_in_even[e].astype(BF16)
            gq = (jnp.tile(g_q[e], H_A) * (HD_A ** -0.5 * LOG2E)).reshape(1, W_A)
            gk = jnp.tile(g_k[e], H_A).reshape(1, W_A)
            bias2 = sb_bias[e] * LOG2E
            bias_hi = bias2.astype(BF16).astype(F32)
            bias_hl = jnp.stack([bias_hi, (bias2 - bias_hi).astype(BF16).astype(F32)])
            wr = _block_diag(w_rgate[e]).astype(BF16)
            wi = _block_diag(w_igate[e]).astype(BF16)
            br = b_rgate[e].reshape(1, W_B)
            bi = b_igate[e].reshape(1, W_B)
            lam = lru_lambda[e].reshape(1, W_B)
            bc = b_conv_lru[e].reshape(1, W_B)
            w_out = w_out_even[e].astype(BF16)
            lru_w = (w_conv_lru[e], bc, wr, br, wi, bi, lam)

            q, kt, vt, kb, vb, xb, gb = _even_in(yp, gm, w_in, head_mean, gq, gk, tm_p, tp)
            oa = _attn_prompt(q.reshape(bp, tp, W_A), kb.reshape(bp, tp, W_A), vb.reshape(bp, tp, W_A),
                              bias_hl, tri, tk, nsub, unroll, chunk)
            xb3 = xb.reshape(bp, tp, W_B)
            y3, hl = _lru_out(xb3, gb.reshape(bp, tp, W_B), jnp.zeros((bp, 1, W_B), F32),
                              jnp.zeros((bp, SUBLANES, W_B), F32), yp.reshape(bp, tp, D_MODEL), oa,
                              *lru_w, w_out, tt_p, False)
            yp = y3.reshape(mp, D_MODEL)
            to_thd = lambda a: jnp.transpose(a.reshape(bp, H_A, HD_A, tp), (0, 3, 1, 2))
            outs["kp"].append(to_thd(kt))
            outs["vp"].append(to_thd(vt))
            outs["hp"].append(hl[:, SUBLANES - 1])
            outs["cp"].append(xb3[:, tp - (CONV_B - 1):])

            q, k, v, kb, vb, xb, gb = _even_in(ys, gm, w_in, head_mean, gq, gk, tm_s, None)
            qbd = jnp.where(qbd_mask, jnp.tile(q.reshape(bs, ts, W_A), (1, H_A, 1)), jnp.zeros((), BF16))
            brow = jnp.repeat(bias2, SUBLANES).reshape(H_A * SUBLANES, 1)
            pad = ((0, 0), (0, 0), (0, PAGE - ts))
            knew = jnp.pad(jnp.swapaxes(kb.reshape(bs, ts, W_A), 1, 2), pad)
            vnew = jnp.pad(jnp.swapaxes(vb.reshape(bs, ts, W_A), 1, 2), pad)
            oa = _attn_sample(page_table, qbd, brow, knew, vnew, tri_s, ck, cv, e, n_group)
            xb3 = xb.reshape(bs, ts, W_B)
            hist = jnp.pad(state_lru_conv[e], ((0, 0), (0, SUBLANES - (CONV_B - 1)), (0, 0))).reshape(1, ms, W_B)
            h0 = jnp.repeat(state_lru_h[e], ts, axis=0).reshape(1, ms, W_B)
            y3, hl = _lru_out(xb.reshape(1, ms, W_B), gb.reshape(1, ms, W_B), h0, hist,
                              ys.reshape(1, ms, D_MODEL), oa.reshape(1, ms, W_A), *lru_w, w_out, ms, True)
            ys = y3.reshape(ms, D_MODEL)
            outs["ks"].append(k.reshape(bs, ts, H_A, HD_A))
            outs["vs"].append(v.reshape(bs, ts, H_A, HD_A))
            outs["hs"].append(hl.reshape(bs, ts, W_B)[:, ts - 1])
            outs["cs"].append(xb3[:, ts - (CONV_B - 1):])
        else:
            o = l // 2
            w_in = w_in_odd[o].astype(BF16)
            gv = g_v[o].reshape(1, W_C)
            w_out = w_out_odd[o].astype(BF16)
            w_tril = jnp.tril(w_spatial[o])
            wmix_p = w_tril.astype(BF16)
            bmix_p = jnp.repeat(b_spatial[o].T, CW_C, axis=1)
            taps = jnp.stack([jnp.pad(jnp.diagonal(w_tril[:, :ts, :ts], offset=-k, axis1=1, axis2=2),
                                      ((0, 0), (k, 0))) for k in range(ts)])
            wmix_s = jnp.repeat(jnp.transpose(taps, (0, 2, 1)), CW_C, axis=2)
            bmix_s = bmix_p[:ts]

            yp, vn = _odd_mixer(yp, gm, w_in, gv, wmix_p, bmix_p, w_out, tm_p, CHUNK)
            outs["chp"].append(vn.reshape(bp, tp, W_C)[:, last_chunk_start:])
            ys, vn = _odd_mixer(ys, gm, w_in, gv, wmix_s, bmix_s, w_out, tm_s, ts)
            outs["chs"].append(vn.reshape(bs, ts, W_C))

        gf = g_ffn[l].reshape(1, D_MODEL)
        yp, gt = _ffn(yp, gf, jnp.zeros((bp, SUBLANES, D_FF), F32), *ffn_w, l, tm_f, tf, tp)
        tiles = tp // tm_f
        outs["fp"].append(gt.reshape(bp, tiles, SUBLANES, D_FF)[:, tiles - 1, SUBLANES - (CONV_F - 1):])
        hist = jnp.pad(state_ffn_conv[l], ((0, 0), (0, SUBLANES - (CONV_F - 1)), (0, 0))).reshape(ms, D_FF)
        ys, gt = _ffn(ys, gf, hist, *ffn_w, l, tm_s, tf, ts)
        outs["fs"].append(gt.reshape(bs, ts, D_FF)[:, ts - (CONV_F - 1):])

    st = lambda n: jnp.stack(outs[n])
    return (yp.reshape(bp, tp, D_MODEL), ys.reshape(bs, ts, D_MODEL),
            st("kp"), st("vp"), st("ks"), st("vs"), st("hp"), st("hs"), st("cp"), st("cs"),
            st("chp"), st("chs"), st("fp"), st("fs"))
```

```python
import functools
import math

import numpy as np
import jax
import jax.numpy as jnp
from jax import lax
from jax.experimental import pallas as pl
from jax.experimental.pallas import tpu as pltpu

F32 = jnp.float32
BF16 = jnp.bfloat16

D_MODEL = 1024
H_A, HD_A = 8, 64
W_A = H_A * HD_A
H_B, BW_B = 8, 64
W_B = H_B * BW_B
CONV_B = 4
C_LRU = 8.0
H_C, CW_C = 8, 128
W_C = H_C * CW_C
CHUNK = 128
D_FF = 3 * D_MODEL
CONV_F = 3
PAGE = 128
EPS = 1e-6
D_IN_EVEN = 3 * W_A + 2 * W_B
LOG2E = math.log2(math.e)

LANES = 128
SUBLANES = 8
VMEM_LIMIT = 56 * 1024 * 1024


def _cparams(sem):
    return pltpu.CompilerParams(dimension_semantics=sem, vmem_limit_bytes=VMEM_LIMIT)


def _rms(x, g):
    ms = jnp.mean(x * x, axis=-1, keepdims=True)
    return x * lax.rsqrt(ms + EPS) * g


def _dot(a, b):
    return jnp.dot(a, b, preferred_element_type=F32)


def _dot_nt(a, b):
    return lax.dot_general(a, b, (((1,), (1,)), ((), ())), preferred_element_type=F32)


def _even_in_kernel(x_ref, g_ref, w_ref, pm_ref, gq_ref, gk_ref, *rest, kv_transposed, n_alias):
    q_ref, k_ref, v_ref, kb_ref, vb_ref, xb_ref, gb_ref = rest[n_alias:]
    xn = _rms(x_ref[...], g_ref[...]).astype(BF16)
    p = _dot(xn, w_ref[...])
    pm = pm_ref[...]

    def head_norm(t, g):
        sq = t * t
        hi = sq.astype(BF16)
        lo = (sq - hi.astype(F32)).astype(BF16)
        ms = _dot(hi, pm) + _dot(lo, pm)
        return t * lax.rsqrt(ms + EPS) * g

    q = head_norm(p[:, 0:W_A], gq_ref[...])
    k = head_norm(p[:, W_A:2 * W_A], gk_ref[...])
    v = p[:, 2 * W_A:3 * W_A]
    q_ref[...] = q.astype(BF16)
    if kv_transposed:
        kt, vt = k.T, v.T
        for s in range(k_ref.shape[0]):
            k_ref[s, 0] = kt
            v_ref[s, 0] = vt
    else:
        k_ref[...] = k
        v_ref[...] = v
    kb_ref[...] = k.astype(BF16)
    vb_ref[...] = v.astype(BF16)
    xb_ref[...] = p[:, 3 * W_A:3 * W_A + W_B]
    gb_ref[...] = p[:, 3 * W_A + W_B:]


def _even_in(x, g, w, pm, gq, gk, tm, seq_len=None, slot=0, n_slots=1, kv_stack=None):
    m = x.shape[0]
    row = lambda i: (i, 0)
    fixed = lambda i: (0, 0)
    f32o = jax.ShapeDtypeStruct((m, W_A), F32)
    bfo = jax.ShapeDtypeStruct((m, W_A), BF16)
    blk = pl.BlockSpec((tm, W_A), row)
    kv_shape, kv_blk = f32o, blk
    extra_in, extra_specs, aliases = (), [], {}
    if seq_len is not None:
        tps = seq_len // tm
        kv_shape = jax.ShapeDtypeStruct((n_slots, m // seq_len, W_A, seq_len), F32)
        if slot == 0:
            kv_blk = pl.BlockSpec((n_slots, 1, W_A, tm), lambda i: (0, i // tps, 0, i % tps))
        else:
            kv_blk = pl.BlockSpec((1, 1, W_A, tm), lambda i: (slot, i // tps, 0, i % tps))
            extra_in = tuple(kv_stack)
            extra_specs = [pl.BlockSpec(memory_space=pl.ANY)] * 2
            aliases = {6: 1, 7: 2}
    return pl.pallas_call(
        functools.partial(_even_in_kernel, kv_transposed=seq_len is not None, n_alias=len(extra_in)),
        grid=(m // tm,),
        in_specs=[pl.BlockSpec((tm, D_MODEL), row), pl.BlockSpec((1, D_MODEL), fixed),
                  pl.BlockSpec((D_MODEL, D_IN_EVEN), fixed), pl.BlockSpec((W_A, W_A), fixed),
                  pl.BlockSpec((1, W_A), fixed), pl.BlockSpec((1, W_A), fixed)] + extra_specs,
        out_specs=[blk, kv_blk, kv_blk, blk, blk, blk, blk],
        out_shape=[bfo, kv_shape, kv_shape, bfo, bfo, f32o, f32o],
        input_output_aliases=aliases,
        compiler_params=_cparams(("parallel",)),
        name="even_in",
    )(x, g, w, pm, gq, gk, *extra_in)


def _softplus2(z):
    return jnp.maximum(z, jnp.log2(1.0 + jnp.exp2(jnp.minimum(z, 100.0))))


def _sb_weights(z, tri, carry, mask):
    sp = _softplus2(z)
    if mask is not None:
        sp = jnp.where(mask, sp, 0.0)
    c = _dot(sp.astype(BF16), tri)
    w = jnp.exp2(z - c - carry)
    if mask is not None:
        w = jnp.where(mask, w, 0.0)
    return w.astype(BF16), carry + c[:, 0:1]


def _attn_prompt_kernel(bias_ref, q_ref, k_ref, v_ref, tri_ref, o_ref, ka_ref, *, tk, nsub, unroll, chunk):
    pair = pl.program_id(1)
    i = pl.program_id(2)
    t = k_ref.shape[1]
    rows = 2 * nsub * tk

    @pl.when(i == 0)
    def _():
        ka_ref[:, 0:LANES] = k_ref[0]
        lane_t = lax.broadcasted_iota(jnp.int32, (t, LANES), 1)
        ka_ref[:, LANES:2 * LANES] = jnp.where(lane_t < 2, 1.0, 0.0).astype(BF16)

    q = q_ref[0]
    tri = tri_ref[...]
    lane = lax.broadcasted_iota(jnp.int32, (tk, LANES), 1)
    first = lane < HD_A
    zero = jnp.zeros((tk, LANES), BF16)
    ext = []
    for h in range(2):
        hi = bias_ref[0, 2 * pair + h]
        lo = bias_ref[1, 2 * pair + h]
        ext.append(jnp.where(lane == 0, hi, jnp.where(lane == 1, lo, 0.0)).astype(BF16))
    chains = []
    for s in range(nsub):
        qs = q[s * tk:(s + 1) * tk]
        chains.append(jnp.concatenate([jnp.where(first, qs, zero), ext[0]], axis=1))
        chains.append(jnp.concatenate([jnp.where(first, zero, qs), ext[1]], axis=1))
    qa = jnp.concatenate(chains, axis=0)

    def sweep(j_hi, nblk, qa_r, carry, acc, diag_rows):
        r = qa_r.shape[0]
        ch = min(chunk, r)
        for u in range(nblk):
            lo = pl.multiple_of((j_hi - u) * tk, tk)
            kj = ka_ref[pl.ds(lo, tk), :]
            vj = v_ref[0, pl.ds(lo, tk), :]
            cs, accs = [], []
            for c0 in range(0, r, ch):
                sl = slice(c0, c0 + ch)
                z = _dot_nt(qa_r[sl], kj)
                mask = None
                if diag_rows and u == 0 and c0 < diag_rows:
                    rr = lax.broadcasted_iota(jnp.int32, (ch, tk), 0) + c0
                    cc = lax.broadcasted_iota(jnp.int32, (ch, tk), 1)
                    mask = (rr >= diag_rows) | (cc < (rr % tk))
                w, c2 = _sb_weights(z, tri, carry[sl], mask)
                cs.append(c2)
                accs.append(acc[sl] + _dot(w, vj))
            carry = cs[0] if len(cs) == 1 else jnp.concatenate(cs, axis=0)
            acc = accs[0] if len(accs) == 1 else jnp.concatenate(accs, axis=0)
        return carry, acc

    carry = jnp.zeros((rows, 1), F32)
    acc = jnp.zeros((rows, LANES), F32)
    for m in reversed(range(nsub)):
        r0 = m * 2 * tk
        c_m, a_m = sweep(nsub * i + m, 1, qa[r0:], carry[r0:], acc[r0:], 2 * tk)
        carry = jnp.concatenate([carry[:r0], c_m], axis=0) if r0 else c_m
        acc = jnp.concatenate([acc[:r0], a_m], axis=0) if r0 else a_m

    nloop = nsub * i
    carry, acc = lax.fori_loop(
        0, nloop // unroll,
        lambda n, st: sweep(nloop - 1 - unroll * n, unroll, qa, st[0], st[1], 0), (carry, acc))
    for s in range(nsub):
        a0 = acc[(2 * s) * tk:(2 * s + 1) * tk]
        a1 = acc[(2 * s + 1) * tk:(2 * s + 2) * tk]
        o_ref[0, s * tk:(s + 1) * tk, :] = jnp.where(first, a0, a1).astype(o_ref.dtype)


def _attn_prompt(q, k, v, bias_hl, tri, tk, nsub, unroll, chunk):
    b, t, _ = q.shape
    npair = W_A // LANES
    tq = tk * nsub
    assert t % tq == 0 and nsub % unroll == 0
    return pl.pallas_call(
        functools.partial(_attn_prompt_kernel, tk=tk, nsub=nsub, unroll=unroll, chunk=chunk),
        grid=(b, npair, t // tq),
        in_specs=[pl.BlockSpec(memory_space=pltpu.SMEM),
                  pl.BlockSpec((1, tq, LANES), lambda bi, p, i: (bi, i, p)),
                  pl.BlockSpec((1, t, LANES), lambda bi, p, i: (bi, 0, p)),
                  pl.BlockSpec((1, t, LANES), lambda bi, p, i: (bi, 0, p)),
                  pl.BlockSpec((tk, tk), lambda bi, p, i: (0, 0))],
        out_specs=pl.BlockSpec((1, tq, LANES), lambda bi, p, i: (bi, i, p)),
        out_shape=jax.ShapeDtypeStruct((b, t, W_A), BF16),
        scratch_shapes=[pltpu.VMEM((t, 2 * LANES), BF16)],
        compiler_params=_cparams(("parallel", "parallel", "arbitrary")),
        name="attn_prompt",
    )(bias_hl, q, k, v, tri)


def _attn_sample_kernel(pt_ref, qbd_ref, brow_ref, kn_ref, vn_ref, tri_ref, *rest, n_group):
    del pt_ref
    pages = rest[:2 * n_group]
    o_ref, acc_ref, car_ref = rest[2 * n_group:]
    p = pl.program_id(1)
    qbd = qbd_ref[0]
    brow = brow_ref[...]
    tri = tri_ref[...]
    rows = H_A * SUBLANES
    blk = 2 * PAGE
    nblk = n_group * PAGE // blk

    @pl.when(p == 0)
    def _():
        z = _dot(qbd, kn_ref[0]) + brow
        rr = lax.broadcasted_iota(jnp.int32, (rows, PAGE), 0) % SUBLANES
        cc = lax.broadcasted_iota(jnp.int32, (rows, PAGE), 1)
        w, carry = _sb_weights(z, tri[0:PAGE, 0:PAGE], jnp.zeros((rows, 1), F32), cc < rr)
        acc_ref[...] = _dot_nt(w, vn_ref[0])
        car_ref[...] = carry

    kall = jnp.concatenate([pages[g][0, 0] for g in range(n_group)], axis=1).astype(BF16)
    vall = jnp.concatenate([pages[n_group + g][0, 0] for g in range(n_group)], axis=1).astype(BF16)
    z = _dot(qbd, kall) + brow
    spb = _softplus2(z).astype(BF16)
    cs = [_dot(spb[:, b * blk:(b + 1) * blk], tri) for b in range(nblk)]
    run = car_ref[...]
    args = [None] * nblk
    for b in reversed(range(nblk)):
        args[b] = z[:, b * blk:(b + 1) * blk] - cs[b] - run
        run = run + cs[b][:, 0:1]
    w = jnp.exp2(jnp.concatenate(args, axis=1)).astype(BF16)
    car_ref[...] = run
    acc_ref[...] += _dot_nt(w, vall)

    @pl.when(p == pl.num_programs(1) - 1)
    def _():
        acc = acc_ref[...]
        lane_head = lax.broadcasted_iota(jnp.int32, (SUBLANES, W_A), 1) // HD_A
        out = jnp.zeros((SUBLANES, W_A), F32)
        for h in range(H_A):
            out = jnp.where(lane_head == h, acc[h * SUBLANES:(h + 1) * SUBLANES, :], out)
        o_ref[0] = out.astype(o_ref.dtype)


def _attn_sample(page_table, qbd, brow, knew, vnew, tri, cache_k, cache_v, e, n_group):
    bs, n_pages = page_table.shape
    steps = n_pages // n_group
    rows = H_A * SUBLANES

    def page_spec(i):
        return pl.BlockSpec(
            (1, 1, W_A, PAGE),
            lambda b, p, pt, i=i: (e, pt[b, (steps - 1 - p) * n_group + i], 0, 0))

    grid_spec = pltpu.PrefetchScalarGridSpec(
        num_scalar_prefetch=1,
        grid=(bs, steps),
        in_specs=[pl.BlockSpec((1, rows, W_A), lambda b, p, pt: (b, 0, 0)),
                  pl.BlockSpec((rows, 1), lambda b, p, pt: (0, 0)),
                  pl.BlockSpec((1, W_A, PAGE), lambda b, p, pt: (b, 0, 0)),
                  pl.BlockSpec((1, W_A, PAGE), lambda b, p, pt: (b, 0, 0)),
                  pl.BlockSpec((2 * PAGE, 2 * PAGE), lambda b, p, pt: (0, 0))]
                 + [page_spec(i) for i in range(n_group)] * 2,
        out_specs=pl.BlockSpec((1, SUBLANES, W_A), lambda b, p, pt: (b, 0, 0)),
        scratch_shapes=[pltpu.VMEM((rows, W_A), F32), pltpu.VMEM((rows, 1), F32)],
    )
    return pl.pallas_call(
        functools.partial(_attn_sample_kernel, n_group=n_group),
        grid_spec=grid_spec,
        out_shape=jax.ShapeDtypeStruct((bs, SUBLANES, W_A), BF16),
        compiler_params=_cparams(("parallel", "arbitrary")),
        name="attn_sample",
    )(page_table, qbd, brow, knew, vnew, tri, *([cache_k] * n_group), *([cache_v] * n_group))


def _lru_kernel(xb_ref, gb_ref, h0_ref, cb_ref, res_ref, oa_ref, wc_ref, bc_ref, wr_ref, br_ref, wi_ref, bi_ref,
                lam_ref, wo_ref, y_ref, hl_ref, xp_ref, hc_ref, *, tt, short_seq):
    t = pl.program_id(1)

    if not short_seq:
        @pl.when(t == 0)
        def _():
            xp_ref[...] = cb_ref[0]
            hc_ref[...] = h0_ref[0]

    x = xb_ref[0]
    row = lax.broadcasted_iota(jnp.int32, (tt, W_B), 0)
    r8 = row % SUBLANES
    wc = wc_ref[...]
    xc = bc_ref[...] + wc[3:4] * x
    if short_seq:
        hist = cb_ref[0]
        for k in range(1, CONV_B):
            shift = (tt + k - (CONV_B - 1)) % tt
            hist_k = pltpu.roll(hist, shift, 0) if shift else hist
            xc = xc + wc[CONV_B - 1 - k:CONV_B - k] * jnp.where(r8 < k, hist_k, pltpu.roll(x, k, 0))
    else:
        prev = xp_ref[...]
        top = lax.broadcasted_iota(jnp.int32, (SUBLANES, W_B), 0)
        for k in range(1, CONV_B):
            xk = pltpu.roll(x, k, 0)
            prev_k = jnp.concatenate([jnp.where(top < k, pltpu.roll(prev, k, 0), xk[0:SUBLANES]), xk[SUBLANES:]],
                                     axis=0)
            xc = xc + wc[CONV_B - 1 - k:CONV_B - k] * prev_k
        xp_ref[...] = x[tt - SUBLANES:tt]

    xcb = xc.astype(BF16)
    r = jax.nn.sigmoid(_dot(xcb, wr_ref[...]) + br_ref[...])
    ig = jax.nn.sigmoid(_dot(xcb, wi_ref[...]) + bi_ref[...])
    nl = -lam_ref[...]
    sp_lam = jnp.maximum(nl, 0.0) + jnp.log1p(jnp.exp(-jnp.abs(nl)))
    log_a = -C_LRU * r * sp_lam
    a = jnp.exp(log_a)
    u = jnp.sqrt((1.0 + a * a) * jnp.tanh(-log_a)) * (ig * xc)

    grp = (tt // SUBLANES, SUBLANES, W_B)
    a, u = a.reshape(grp), u.reshape(grp)
    r8g = lax.broadcasted_iota(jnp.int32, grp, 1)
    for d in (1, 2, 4):
        keep = r8g >= d
        a_sh = jnp.where(keep, pltpu.roll(a, d, 1), 1.0)
        u_sh = jnp.where(keep, pltpu.roll(u, d, 1), 0.0)
        u = a * u_sh + u
        a = a * a_sh
    a, u = a.reshape(tt, W_B), u.reshape(tt, W_B)
    if short_seq:
        h = a * h0_ref[0] + u
        hl_ref[0] = h
    else:
        h_in = hc_ref[...]
        parts = []
        for g in range(tt // SUBLANES):
            sl = slice(g * SUBLANES, (g + 1) * SUBLANES)
            h_g = a[sl] * h_in + u[sl]
            h_in = h_g[SUBLANES - 1:SUBLANES]
            parts.append(h_g)
        h = jnp.concatenate(parts, axis=0)
        hc_ref[...] = h_in
        hl_ref[0] = parts[-1]
    ob = (h * jax.nn.gelu(gb_ref[0])).astype(BF16)
    y_ref[0] = (res_ref[0] + _dot(oa_ref[0], wo_ref[0:W_A, :]) + _dot(ob, wo_ref[W_A:W_A + W_B, :]))


def _lru_out(xb, gb, h0, cbuf, res, oa, wc, bc, wr, br, wi, bi, lam, wo, tt, short_seq):
    b, t, _ = xb.shape
    seq = lambda bi, ti: (bi, ti, 0)
    per_b = lambda bi, ti: (bi, 0, 0)
    fixed = lambda bi, ti: (0, 0)
    if short_seq:
        assert t == tt
        h0_spec, cb_spec = pl.BlockSpec((1, tt, W_B), seq), pl.BlockSpec((1, tt, W_B), seq)
        hl_spec, hl_shape = pl.BlockSpec((1, tt, W_B), seq), jax.ShapeDtypeStruct((b, t, W_B), F32)
    else:
        h0_spec, cb_spec = pl.BlockSpec((1, 1, W_B), per_b), pl.BlockSpec((1, SUBLANES, W_B), per_b)
        hl_spec, hl_shape = pl.BlockSpec((1, SUBLANES, W_B), per_b), jax.ShapeDtypeStruct((b, SUBLANES, W_B), F32)
    return pl.pallas_call(
        functools.partial(_lru_kernel, tt=tt, short_seq=short_seq),
        grid=(b, t // tt),
        in_specs=[pl.BlockSpec((1, tt, W_B), seq), pl.BlockSpec((1, tt, W_B), seq), h0_spec, cb_spec,
                  pl.BlockSpec((1, tt, D_MODEL), seq), pl.BlockSpec((1, tt, W_A), seq),
                  pl.BlockSpec((CONV_B, W_B), fixed), pl.BlockSpec((1, W_B), fixed),
                  pl.BlockSpec((W_B, W_B), fixed), pl.BlockSpec((1, W_B), fixed),
                  pl.BlockSpec((W_B, W_B), fixed), pl.BlockSpec((1, W_B), fixed),
                  pl.BlockSpec((1, W_B), fixed), pl.BlockSpec((W_A + W_B, D_MODEL), fixed)],
        out_specs=[pl.BlockSpec((1, tt, D_MODEL), seq), hl_spec],
        out_shape=[jax.ShapeDtypeStruct((b, t, D_MODEL), F32), hl_shape],
        scratch_shapes=[pltpu.VMEM((SUBLANES, W_B), F32), pltpu.VMEM((1, W_B), F32)],
        compiler_params=_cparams(("parallel", "arbitrary")),
        name="rglru_out",
    )(xb, gb, h0, cbuf, res, oa, wc, bc, wr, br, wi, bi, lam, wo)


def _odd_kernel(x_ref, g_ref, win_ref, gv_ref, wmix_ref, bmix_ref, wout_ref, y_ref, vn_ref, um_ref,
                *, tm, rblk):
    x = x_ref[...]
    xn = _rms(x, g_ref[...]).astype(BF16)
    z = jax.nn.gelu(_dot(xn, win_ref[...]))
    for h in range(H_C):
        lo, hi = h * CW_C, (h + 1) * CW_C
        vh = z[:, W_C + lo:W_C + hi]
        ms = jnp.mean(vh * vh, axis=-1, keepdims=True)
        vnh = vh * lax.rsqrt(ms + EPS) * gv_ref[:, lo:hi]
        vn_ref[:, lo:hi] = vnh
        if rblk == SUBLANES:
            v3 = vnh.reshape(tm // SUBLANES, SUBLANES, CW_C)
            mixed = wmix_ref[0, :, lo:hi][None] * v3 + bmix_ref[:, lo:hi][None]
            for kk in range(1, SUBLANES):
                mixed = mixed + wmix_ref[kk, :, lo:hi][None] * pltpu.roll(v3, kk, 1)
            um_ref[:, lo:hi] = (z[:, lo:hi] * mixed.reshape(tm, CW_C)).astype(BF16)
        else:
            vnb = vnh.astype(BF16)
            wm = wmix_ref[h]
            for c in range(tm // rblk):
                r0, r1 = c * rblk, (c + 1) * rblk
                mixed = _dot(wm, vnb[r0:r1]) + bmix_ref[:, lo:hi]
                um_ref[r0:r1, lo:hi] = (z[r0:r1, lo:hi] * mixed).astype(BF16)
    y_ref[...] = x + _dot(um_ref[...], wout_ref[...])


def _odd_mixer(x, g, win, gv, wmix, bmix, wout, tm, rblk):
    m = x.shape[0]
    row = lambda i: (i, 0)
    fixed = lambda i: (0, 0)
    return pl.pallas_call(
        functools.partial(_odd_kernel, tm=tm, rblk=rblk),
        grid=(m // tm,),
        in_specs=[pl.BlockSpec((tm, D_MODEL), row), pl.BlockSpec((1, D_MODEL), fixed),
                  pl.BlockSpec((D_MODEL, 2 * W_C), fixed), pl.BlockSpec((1, W_C), fixed),
                  pl.BlockSpec(wmix.shape, lambda i: (0, 0, 0)),
                  pl.BlockSpec((rblk, W_C), fixed), pl.BlockSpec((W_C, D_MODEL), fixed)],
        out_specs=[pl.BlockSpec((tm, D_MODEL), row), pl.BlockSpec((tm, W_C), row)],
        out_shape=[jax.ShapeDtypeStruct((m, D_MODEL), F32), jax.ShapeDtypeStruct((m, W_C), F32)],
        scratch_shapes=[pltpu.VMEM((tm, W_C), BF16)],
        compiler_params=_cparams(("parallel",)),
        name="odd_mixer",
    )(x, g, win, gv, wmix, bmix, wout)


def _ffn_kernel(x_ref, g_ref, init_ref, wg_ref, wu_ref, wc_ref, bc_ref, wd_ref,
                y_ref, gt_ref, xn_ref, car_ref, *, tm, tiles_per_seq, short_seq):
    m = pl.program_id(0)
    j = pl.program_id(1)

    @pl.when(j == 0)
    def _():
        x = x_ref[...]
        xn_ref[...] = _rms(x, g_ref[...]).astype(BF16)
        y_ref[...] = x

    if not short_seq:
        @pl.when(m % tiles_per_seq == 0)
        def _():
            car_ref[j] = init_ref[0]

    xn = xn_ref[...]
    g = _dot(xn, wg_ref[...])
    u = _dot(xn, wu_ref[...])
    g1 = pltpu.roll(g, 1, 0)
    g2 = pltpu.roll(g, 2, 0)
    if short_seq:
        hist = init_ref[...]
        r8 = lax.broadcasted_iota(jnp.int32, g.shape, 0) % SUBLANES
        p1 = jnp.where(r8 == 0, pltpu.roll(hist, tm - 1, 0), g1)
        p2 = jnp.where(r8 < 2, hist, g2)
        gt_ref[...] = g
    else:
        prev = car_ref[j]
        r8 = lax.broadcasted_iota(jnp.int32, prev.shape, 0)
        p1 = jnp.concatenate([jnp.where(r8 < 1, pltpu.roll(prev, 1, 0), g1[0:SUBLANES]), g1[SUBLANES:]], axis=0)
        p2 = jnp.concatenate([jnp.where(r8 < 2, pltpu.roll(prev, 2, 0), g2[0:SUBLANES]), g2[SUBLANES:]], axis=0)
        tail = g[tm - SUBLANES:tm]
        car_ref[j] = tail
        gt_ref[0] = tail
    wc = wc_ref[...]
    gc = bc_ref[...] + wc[2:3] * g + wc[1:2] * p1 + wc[0:1] * p2
    hid = (jax.nn.gelu(gc) * u).astype(BF16)
    y_ref[...] += _dot(hid, wd_ref[...])


def _ffn(x, g, init, wg, wu, wc, bc, wd, layer, tm, tf, seq_len):
    m = x.shape[0]
    nff = D_FF // tf
    short_seq = seq_len < tm
    if short_seq:
        assert seq_len == SUBLANES and m == tm
        tiles_per_seq = 1
        init_spec = pl.BlockSpec((tm, tf), lambda i, j: (i, j))
        gt_spec = pl.BlockSpec((tm, tf), lambda i, j: (i, j))
        gt_shape = jax.ShapeDtypeStruct((m, D_FF), F32)
    else:
        assert seq_len % tm == 0
        tiles_per_seq = seq_len // tm
        init_spec = pl.BlockSpec((1, SUBLANES, tf), lambda i, j: (i // tiles_per_seq, 0, j))
        gt_spec = pl.BlockSpec((1, SUBLANES, tf), lambda i, j: (i, 0, j))
        gt_shape = jax.ShapeDtypeStruct((m // tm, SUBLANES, D_FF), F32)
    return pl.pallas_call(
        functools.partial(_ffn_kernel, tm=tm, tiles_per_seq=tiles_per_seq, short_seq=short_seq),
        grid=(m // tm, nff),
        in_specs=[pl.BlockSpec((tm, D_MODEL), lambda i, j: (i, 0)),
                  pl.BlockSpec((1, D_MODEL), lambda i, j: (0, 0)),
                  init_spec,
                  pl.BlockSpec((None, D_MODEL, tf), lambda i, j: (layer, 0, j)),
                  pl.BlockSpec((None, D_MODEL, tf), lambda i, j: (layer, 0, j)),
                  pl.BlockSpec((None, CONV_F, tf), lambda i, j: (layer, 0, j)),
                  pl.BlockSpec((None, 1, tf), lambda i, j: (layer, 0, j)),
                  pl.BlockSpec((None, tf, D_MODEL), lambda i, j: (layer, j, 0))],
        out_specs=[pl.BlockSpec((tm, D_MODEL), lambda i, j: (i, 0)), gt_spec],
        out_shape=[jax.ShapeDtypeStruct((m, D_MODEL), F32), gt_shape],
        scratch_shapes=[pltpu.VMEM((tm, D_MODEL), BF16), pltpu.VMEM((nff, SUBLANES, tf), F32)],
        compiler_params=_cparams(("arbitrary", "arbitrary")),
        name="conv_ffn",
    )(x, g, init, wg, wu, wc, bc, wd)


def _block_diag(w):
    n, c, d = w.shape
    eye = jnp.eye(n, dtype=w.dtype)
    return (eye[:, None, :, None] * w[:, :, None, :]).reshape(n * c, n * d)


def _suffix_tri(n):
    j = np.arange(n)[:, None]
    s = np.arange(n)[None, :]
    return jnp.asarray((j >= s).astype(np.float32), dtype=BF16)


def kernel(x_prompt, x_sample, cache_k, cache_v, state_lru_h, state_lru_conv, state_ffn_conv, page_table,
           g_mix, g_ffn, w_in_even, g_q, g_k, sb_bias, w_conv_lru, b_conv_lru, w_rgate, b_rgate, w_igate,
           b_igate, lru_lambda, w_out_even, w_in_odd, g_v, w_spatial, b_spatial, w_out_odd,
           w_gate, w_up, w_ffn_conv, b_ffn_conv, w_down):
    bp, tp, _ = x_prompt.shape
    bs, ts, _ = x_sample.shape
    depth = g_mix.shape[0]
    n_pool = cache_k.shape[1]
    mp, ms = bp * tp, bs * ts
    assert ts == SUBLANES and tp % 512 == 0

    tm_p, tm_s = 512, ms
    tm_f = 1024
    tk, nsub, unroll, chunk = 256, 8, 8, 1024
    tt_p = 256
    tf = 1024
    n_group = 32

    yp = x_prompt.reshape(mp, D_MODEL)
    ys = x_sample.reshape(ms, D_MODEL)

    head_mean = jnp.asarray(np.kron(np.eye(H_A), np.full((HD_A, HD_A), 1.0 / HD_A)), dtype=BF16)
    tri = _suffix_tri(tk)
    tri_s = _suffix_tri(2 * PAGE)
    ck = jnp.transpose(cache_k, (0, 1, 3, 4, 2)).reshape(cache_k.shape[0], n_pool, W_A, PAGE)
    cv = jnp.transpose(cache_v, (0, 1, 3, 4, 2)).reshape(cache_v.shape[0], n_pool, W_A, PAGE)
    row_head = np.arange(H_A * SUBLANES)[:, None] // SUBLANES
    lane_head = np.arange(W_A)[None, :] // HD_A
    qbd_mask = jnp.asarray(row_head == lane_head)

    ffn_w = (w_gate.astype(BF16), w_up.astype(BF16), w_ffn_conv, b_ffn_conv.reshape(depth, 1, D_FF),
             w_down.astype(BF16))
    n_even = (depth + 1) // 2
    kv_stack = None
    outs = {n: [] for n in ("ks", "vs", "hp", "hs", "cp", "cs", "chp", "chs", "fp", "fs")}
    last_chunk_start = ((tp - 1) // CHUNK) * CHUNK

    for l in range(depth):
        gm = g_mix[l].reshape(1, D_MODEL)
        if l % 2 == 0:
            e = l // 2
            w_in = w_in_even[e].astype(BF16)
            gq = (jnp.tile(g_q[e], H_A) * (HD_A ** -0.5 * LOG2E)).reshape(1, W_A)
            gk = jnp.tile(g_k[e], H_A).reshape(1, W_A)
            bias2 = sb_bias[e] * LOG2E
            bias_hi = bias2.astype(BF16).astype(F32)
            bias_hl = jnp.stack([bias_hi, (bias2 - bias_hi).astype(BF16).astype(F32)])
            wr = _block_diag(w_rgate[e]).astype(BF16)
            wi = _block_diag(w_igate[e]).astype(BF16)
            br = b_rgate[e].reshape(1, W_B)
            bi = b_igate[e].reshape(1, W_B)
            lam = lru_lambda[e].reshape(1, W_B)
            bc = b_conv_lru[e].reshape(1, W_B)
            w_out = w_out_even[e].astype(BF16)
            lru_w = (w_conv_lru[e], bc, wr, br, wi, bi, lam)

            q, kt, vt, kb, vb, xb, gb = _even_in(yp, gm, w_in, head_mean, gq, gk, tm_p, tp, e, n_even, kv_stack)
            kv_stack = (kt, vt)
            oa = _attn_prompt(q.reshape(bp, tp, W_A), kb.reshape(bp, tp, W_A), vb.reshape(bp, tp, W_A),
                              bias_hl, tri, tk, nsub, unroll, chunk)
            xb3 = xb.reshape(bp, tp, W_B)
            y3, hl = _lru_out(xb3, gb.reshape(bp, tp, W_B), jnp.zeros((bp, 1, W_B), F32),
                              jnp.zeros((bp, SUBLANES, W_B), F32), yp.reshape(bp, tp, D_MODEL), oa,
                              *lru_w, w_out, tt_p, False)
            yp = y3.reshape(mp, D_MODEL)
            outs["hp"].append(hl[:, SUBLANES - 1])
            outs["cp"].append(xb3[:, tp - (CONV_B - 1):])

            q, k, v, kb, vb, xb, gb = _even_in(ys, gm, w_in, head_mean, gq, gk, tm_s)
            qbd = jnp.where(qbd_mask, jnp.tile(q.reshape(bs, ts, W_A), (1, H_A, 1)), jnp.zeros((), BF16))
            brow = jnp.repeat(bias2, SUBLANES).reshape(H_A * SUBLANES, 1)
            pad = ((0, 0), (0, 0), (0, PAGE - ts))
            knew = jnp.pad(jnp.swapaxes(kb.reshape(bs, ts, W_A), 1, 2), pad)
            vnew = jnp.pad(jnp.swapaxes(vb.reshape(bs, ts, W_A), 1, 2), pad)
            oa = _attn_sample(page_table, qbd, brow, knew, vnew, tri_s, ck, cv, e, n_group)
            xb3 = xb.reshape(bs, ts, W_B)
            hist = jnp.pad(state_lru_conv[e], ((0, 0), (0, SUBLANES - (CONV_B - 1)), (0, 0))).reshape(1, ms, W_B)
            h0 = jnp.repeat(state_lru_h[e], ts, axis=0).reshape(1, ms, W_B)
            y3, hl = _lru_out(xb.reshape(1, ms, W_B), gb.reshape(1, ms, W_B), h0, hist,
                              ys.reshape(1, ms, D_MODEL), oa.reshape(1, ms, W_A), *lru_w, w_out, ms, True)
            ys = y3.reshape(ms, D_MODEL)
            outs["ks"].append(k.reshape(bs, ts, H_A, HD_A))
            outs["vs"].append(v.reshape(bs, ts, H_A, HD_A))
            outs["hs"].append(hl.reshape(bs, ts, W_B)[:, ts - 1])
            outs["cs"].append(xb3[:, ts - (CONV_B - 1):])
        else:
            o = l // 2
            w_in = w_in_odd[o].astype(BF16)
            gv = g_v[o].reshape(1, W_C)
            w_out = w_out_odd[o].astype(BF16)
            w_tril = jnp.tril(w_spatial[o])
            wmix_p = w_tril.astype(BF16)
            bmix_p = jnp.repeat(b_spatial[o].T, CW_C, axis=1)
            taps = jnp.stack([jnp.pad(jnp.diagonal(w_tril[:, :ts, :ts], offset=-k, axis1=1, axis2=2),
                                      ((0, 0), (k, 0))) for k in range(ts)])
            wmix_s = jnp.repeat(jnp.transpose(taps, (0, 2, 1)), CW_C, axis=2)
            bmix_s = bmix_p[:ts]

            yp, vn = _odd_mixer(yp, gm, w_in, gv, wmix_p, bmix_p, w_out, tm_p, CHUNK)
            outs["chp"].append(vn.reshape(bp, tp, W_C)[:, last_chunk_start:])
            ys, vn = _odd_mixer(ys, gm, w_in, gv, wmix_s, bmix_s, w_out, tm_s, ts)
            outs["chs"].append(vn.reshape(bs, ts, W_C))

        gf = g_ffn[l].reshape(1, D_MODEL)
        yp, gt = _ffn(yp, gf, jnp.zeros((bp, SUBLANES, D_FF), F32), *ffn_w, l, tm_f, tf, tp)
        tiles = tp // tm_f
        outs["fp"].append(gt.reshape(bp, tiles, SUBLANES, D_FF)[:, tiles - 1, SUBLANES - (CONV_F - 1):])
        hist = jnp.pad(state_ffn_conv[l], ((0, 0), (0, SUBLANES - (CONV_F - 1)), (0, 0))).reshape(ms, D_FF)
        ys, gt = _ffn(ys, gf, hist, *ffn_w, l, tm_s, tf, ts)
        outs["fs"].append(gt.reshape(bs, ts, D_FF)[:, ts - (CONV_F - 1):])

    st = lambda n: jnp.stack(outs[n])
    to_thd = lambda a: jnp.transpose(a.reshape(n_even, bp, H_A, HD_A, tp), (0, 1, 4, 2, 3))
    return (yp.reshape(bp, tp, D_MODEL), ys.reshape(bs, ts, D_MODEL),
            to_thd(kv_stack[0]), to_thd(kv_stack[1]), st("ks"), st("vs"), st("hp"), st("hs"), st("cp"), st("cs"),
            st("chp"), st("chs"), st("fp"), st("fs"))
```

```python
import functools
import math

import numpy as np
import jax
import jax.numpy as jnp
from jax import lax
from jax.experimental import pallas as pl
from jax.experimental.pallas import tpu as pltpu

F32 = jnp.float32
BF16 = jnp.bfloat16

D_MODEL = 1024
H_A, HD_A = 8, 64
W_A = H_A * HD_A
H_B, BW_B = 8, 64
W_B = H_B * BW_B
CONV_B = 4
C_LRU = 8.0
H_C, CW_C = 8, 128
W_C = H_C * CW_C
CHUNK = 128
D_FF = 3 * D_MODEL
CONV_F = 3
PAGE = 128
EPS = 1e-6
D_IN_EVEN = 3 * W_A + 2 * W_B
LOG2E = math.log2(math.e)

LANES = 128
SUBLANES = 8
VMEM_LIMIT = 56 * 1024 * 1024


def _cparams(sem):
    return pltpu.CompilerParams(dimension_semantics=sem, vmem_limit_bytes=VMEM_LIMIT)


def _rms(x, g):
    ms = jnp.mean(x * x, axis=-1, keepdims=True)
    return x * lax.rsqrt(ms + EPS) * g


def _dot(a, b):
    return jnp.dot(a, b, preferred_element_type=F32)


def _dot_nt(a, b):
    return lax.dot_general(a, b, (((1,), (1,)), ((), ())), preferred_element_type=F32)


def _even_in_kernel(x_ref, g_ref, w_ref, pm_ref, gq_ref, gk_ref, *rest, kv_transposed, n_alias):
    q_ref, k_ref, v_ref, kb_ref, vb_ref, xb_ref, gb_ref = rest[n_alias:]
    xn = _rms(x_ref[...], g_ref[...]).astype(BF16)
    p = _dot(xn, w_ref[...])
    pm = pm_ref[...]

    def head_norm(t, g):
        ms = _dot((t * t).astype(BF16), pm)
        return t * lax.rsqrt(ms + EPS) * g

    q = head_norm(p[:, 0:W_A], gq_ref[...])
    k = head_norm(p[:, W_A:2 * W_A], gk_ref[...])
    v = p[:, 2 * W_A:3 * W_A]
    q_ref[...] = q.astype(BF16)
    if kv_transposed:
        kt, vt = k.T, v.T
        for s in range(k_ref.shape[0]):
            k_ref[s, 0] = kt
            v_ref[s, 0] = vt
    else:
        k_ref[...] = k
        v_ref[...] = v
    kb_ref[...] = k.astype(BF16)
    vb_ref[...] = v.astype(BF16)
    xb_ref[...] = p[:, 3 * W_A:3 * W_A + W_B]
    gb_ref[...] = p[:, 3 * W_A + W_B:]


def _even_in(x, g, w, pm, gq, gk, tm, seq_len=None, slot=0, n_slots=1, kv_stack=None):
    m = x.shape[0]
    row = lambda i: (i, 0)
    fixed = lambda i: (0, 0)
    f32o = jax.ShapeDtypeStruct((m, W_A), F32)
    bfo = jax.ShapeDtypeStruct((m, W_A), BF16)
    blk = pl.BlockSpec((tm, W_A), row)
    kv_shape, kv_blk = f32o, blk
    extra_in, extra_specs, aliases = (), [], {}
    if seq_len is not None:
        tps = seq_len // tm
        kv_shape = jax.ShapeDtypeStruct((n_slots, m // seq_len, W_A, seq_len), F32)
        if slot == 0:
            kv_blk = pl.BlockSpec((n_slots, 1, W_A, tm), lambda i: (0, i // tps, 0, i % tps))
        else:
            kv_blk = pl.BlockSpec((1, 1, W_A, tm), lambda i: (slot, i // tps, 0, i % tps))
            extra_in = tuple(kv_stack)
            extra_specs = [pl.BlockSpec(memory_space=pl.ANY)] * 2
            aliases = {6: 1, 7: 2}
    return pl.pallas_call(
        functools.partial(_even_in_kernel, kv_transposed=seq_len is not None, n_alias=len(extra_in)),
        grid=(m // tm,),
        in_specs=[pl.BlockSpec((tm, D_MODEL), row), pl.BlockSpec((1, D_MODEL), fixed),
                  pl.BlockSpec((D_MODEL, D_IN_EVEN), fixed), pl.BlockSpec((W_A, W_A), fixed),
                  pl.BlockSpec((1, W_A), fixed), pl.BlockSpec((1, W_A), fixed)] + extra_specs,
        out_specs=[blk, kv_blk, kv_blk, blk, blk, blk, blk],
        out_shape=[bfo, kv_shape, kv_shape, bfo, bfo, f32o, f32o],
        input_output_aliases=aliases,
        compiler_params=_cparams(("parallel",)),
        name="even_in",
    )(x, g, w, pm, gq, gk, *extra_in)


def _softplus2(z):
    return jnp.maximum(z, jnp.log2(1.0 + jnp.exp2(jnp.minimum(z, 100.0))))


def _sb_weights(z, tri, carry, mask):
    sp = _softplus2(z)
    if mask is not None:
        sp = jnp.where(mask, sp, 0.0)
    c = _dot(sp.astype(BF16), tri)
    w = jnp.exp2(z - c - carry)
    if mask is not None:
        w = jnp.where(mask, w, 0.0)
    return w.astype(BF16), carry + c[:, 0:1]


def _attn_prompt_kernel(bias_ref, q_ref, k_ref, v_ref, tri_ref, o_ref, ka_ref, *, tk, nsub, unroll, chunk):
    pair = pl.program_id(1)
    i = pl.program_id(2)
    t = k_ref.shape[1]
    rows = 2 * nsub * tk

    @pl.when(i == 0)
    def _():
        ka_ref[:, 0:LANES] = k_ref[0]
        lane_t = lax.broadcasted_iota(jnp.int32, (t, LANES), 1)
        ka_ref[:, LANES:2 * LANES] = jnp.where(lane_t < 2, 1.0, 0.0).astype(BF16)

    q = q_ref[0]
    tri = tri_ref[...]
    lane = lax.broadcasted_iota(jnp.int32, (tk, LANES), 1)
    first = lane < HD_A
    zero = jnp.zeros((tk, LANES), BF16)
    ext = []
    for h in range(2):
        hi = bias_ref[0, 2 * pair + h]
        lo = bias_ref[1, 2 * pair + h]
        ext.append(jnp.where(lane == 0, hi, jnp.where(lane == 1, lo, 0.0)).astype(BF16))
    chains = []
    for s in range(nsub):
        qs = q[s * tk:(s + 1) * tk]
        chains.append(jnp.concatenate([jnp.where(first, qs, zero), ext[0]], axis=1))
        chains.append(jnp.concatenate([jnp.where(first, zero, qs), ext[1]], axis=1))
    qa = jnp.concatenate(chains, axis=0)

    def sweep(j_hi, nblk, qa_r, carry, acc, diag_rows):
        r = qa_r.shape[0]
        ch = min(chunk, r)
        for u in range(nblk):
            lo = pl.multiple_of((j_hi - u) * tk, tk)
            kj = ka_ref[pl.ds(lo, tk), :]
            vj = v_ref[0, pl.ds(lo, tk), :]
            cs, accs = [], []
            for c0 in range(0, r, ch):
                sl = slice(c0, c0 + ch)
                z = _dot_nt(qa_r[sl], kj)
                mask = None
                if diag_rows and u == 0 and c0 < diag_rows:
                    rr = lax.broadcasted_iota(jnp.int32, (ch, tk), 0) + c0
                    cc = lax.broadcasted_iota(jnp.int32, (ch, tk), 1)
                    mask = (rr >= diag_rows) | (cc < (rr % tk))
                w, c2 = _sb_weights(z, tri, carry[sl], mask)
                cs.append(c2)
                accs.append(acc[sl] + _dot(w, vj))
            carry = cs[0] if len(cs) == 1 else jnp.concatenate(cs, axis=0)
            acc = accs[0] if len(accs) == 1 else jnp.concatenate(accs, axis=0)
        return carry, acc

    carry = jnp.zeros((rows, 1), F32)
    acc = jnp.zeros((rows, LANES), F32)
    for m in reversed(range(nsub)):
        r0 = m * 2 * tk
        c_m, a_m = sweep(nsub * i + m, 1, qa[r0:], carry[r0:], acc[r0:], 2 * tk)
        carry = jnp.concatenate([carry[:r0], c_m], axis=0) if r0 else c_m
        acc = jnp.concatenate([acc[:r0], a_m], axis=0) if r0 else a_m

    nloop = nsub * i
    carry, acc = lax.fori_loop(
        0, nloop // unroll,
        lambda n, st: sweep(nloop - 1 - unroll * n, unroll, qa, st[0], st[1], 0), (carry, acc))
    for s in range(nsub):
        a0 = acc[(2 * s) * tk:(2 * s + 1) * tk]
        a1 = acc[(2 * s + 1) * tk:(2 * s + 2) * tk]
        o_ref[0, s * tk:(s + 1) * tk, :] = jnp.where(first, a0, a1).astype(o_ref.dtype)


def _attn_prompt(q, k, v, bias_hl, tri, tk, nsub, unroll, chunk):
    b, t, _ = q.shape
    npair = W_A // LANES
    tq = tk * nsub
    assert t % tq == 0 and nsub % unroll == 0
    return pl.pallas_call(
        functools.partial(_attn_prompt_kernel, tk=tk, nsub=nsub, unroll=unroll, chunk=chunk),
        grid=(b, npair, t // tq),
        in_specs=[pl.BlockSpec(memory_space=pltpu.SMEM),
                  pl.BlockSpec((1, tq, LANES), lambda bi, p, i: (bi, i, p)),
                  pl.BlockSpec((1, t, LANES), lambda bi, p, i: (bi, 0, p)),
                  pl.BlockSpec((1, t, LANES), lambda bi, p, i: (bi, 0, p)),
                  pl.BlockSpec((tk, tk), lambda bi, p, i: (0, 0))],
        out_specs=pl.BlockSpec((1, tq, LANES), lambda bi, p, i: (bi, i, p)),
        out_shape=jax.ShapeDtypeStruct((b, t, W_A), BF16),
        scratch_shapes=[pltpu.VMEM((t, 2 * LANES), BF16)],
        compiler_params=_cparams(("parallel", "parallel", "arbitrary")),
        name="attn_prompt",
    )(bias_hl, q, k, v, tri)


def _attn_sample_kernel(pt_ref, qbd_ref, brow_ref, kn_ref, vn_ref, tri_ref, *rest, n_group):
    del pt_ref
    pages = rest[:2 * n_group]
    o_ref, acc_ref, car_ref = rest[2 * n_group:]
    p = pl.program_id(1)
    qbd = qbd_ref[0]
    brow = brow_ref[...]
    tri = tri_ref[...]
    rows = H_A * SUBLANES
    blk = 2 * PAGE
    nblk = n_group * PAGE // blk

    @pl.when(p == 0)
    def _():
        z = _dot(qbd, kn_ref[0]) + brow
        rr = lax.broadcasted_iota(jnp.int32, (rows, PAGE), 0) % SUBLANES
        cc = lax.broadcasted_iota(jnp.int32, (rows, PAGE), 1)
        w, carry = _sb_weights(z, tri[0:PAGE, 0:PAGE], jnp.zeros((rows, 1), F32), cc < rr)
        acc_ref[...] = _dot_nt(w, vn_ref[0])
        car_ref[...] = carry

    kall = jnp.concatenate([pages[g][0, 0] for g in range(n_group)], axis=1).astype(BF16)
    vall = jnp.concatenate([pages[n_group + g][0, 0] for g in range(n_group)], axis=1).astype(BF16)
    z = _dot(qbd, kall) + brow
    spb = _softplus2(z).astype(BF16)
    cs = [_dot(spb[:, b * blk:(b + 1) * blk], tri) for b in range(nblk)]
    run = car_ref[...]
    args = [None] * nblk
    for b in reversed(range(nblk)):
        args[b] = z[:, b * blk:(b + 1) * blk] - cs[b] - run
        run = run + cs[b][:, 0:1]
    w = jnp.exp2(jnp.concatenate(args, axis=1)).astype(BF16)
    car_ref[...] = run
    acc_ref[...] += _dot_nt(w, vall)

    @pl.when(p == pl.num_programs(1) - 1)
    def _():
        acc = acc_ref[...]
        lane_head = lax.broadcasted_iota(jnp.int32, (SUBLANES, W_A), 1) // HD_A
        out = jnp.zeros((SUBLANES, W_A), F32)
        for h in range(H_A):
            out = jnp.where(lane_head == h, acc[h * SUBLANES:(h + 1) * SUBLANES, :], out)
        o_ref[0] = out.astype(o_ref.dtype)


def _attn_sample(page_table, qbd, brow, knew, vnew, tri, cache_k, cache_v, e, n_group):
    bs, n_pages = page_table.shape
    steps = n_pages // n_group
    rows = H_A * SUBLANES

    def page_spec(i):
        return pl.BlockSpec(
            (1, 1, W_A, PAGE),
            lambda b, p, pt, i=i: (e, pt[b, (steps - 1 - p) * n_group + i], 0, 0))

    grid_spec = pltpu.PrefetchScalarGridSpec(
        num_scalar_prefetch=1,
        grid=(bs, steps),
        in_specs=[pl.BlockSpec((1, rows, W_A), lambda b, p, pt: (b, 0, 0)),
                  pl.BlockSpec((rows, 1), lambda b, p, pt: (0, 0)),
                  pl.BlockSpec((1, W_A, PAGE), lambda b, p, pt: (b, 0, 0)),
                  pl.BlockSpec((1, W_A, PAGE), lambda b, p, pt: (b, 0, 0)),
                  pl.BlockSpec((2 * PAGE, 2 * PAGE), lambda b, p, pt: (0, 0))]
                 + [page_spec(i) for i in range(n_group)] * 2,
        out_specs=pl.BlockSpec((1, SUBLANES, W_A), lambda b, p, pt: (b, 0, 0)),
        scratch_shapes=[pltpu.VMEM((rows, W_A), F32), pltpu.VMEM((rows, 1), F32)],
    )
    return pl.pallas_call(
        functools.partial(_attn_sample_kernel, n_group=n_group),
        grid_spec=grid_spec,
        out_shape=jax.ShapeDtypeStruct((bs, SUBLANES, W_A), BF16),
        compiler_params=_cparams(("parallel", "arbitrary")),
        name="attn_sample",
    )(page_table, qbd, brow, knew, vnew, tri, *([cache_k] * n_group), *([cache_v] * n_group))


def _lru_kernel(xb_ref, gb_ref, h0_ref, cb_ref, res_ref, oa_ref, wc_ref, bc_ref, wr_ref, br_ref, wi_ref, bi_ref,
                lam_ref, wo_ref, y_ref, hl_ref, xp_ref, hc_ref, *, tt, short_seq):
    t = pl.program_id(1)

    if not short_seq:
        @pl.when(t == 0)
        def _():
            xp_ref[...] = cb_ref[0]
            hc_ref[...] = h0_ref[0]

    x = xb_ref[0]
    row = lax.broadcasted_iota(jnp.int32, (tt, W_B), 0)
    r8 = row % SUBLANES
    wc = wc_ref[...]
    xc = bc_ref[...] + wc[3:4] * x
    if short_seq:
        hist = cb_ref[0]
        for k in range(1, CONV_B):
            shift = (tt + k - (CONV_B - 1)) % tt
            hist_k = pltpu.roll(hist, shift, 0) if shift else hist
            xc = xc + wc[CONV_B - 1 - k:CONV_B - k] * jnp.where(r8 < k, hist_k, pltpu.roll(x, k, 0))
    else:
        prev = xp_ref[...]
        top = lax.broadcasted_iota(jnp.int32, (SUBLANES, W_B), 0)
        for k in range(1, CONV_B):
            xk = pltpu.roll(x, k, 0)
            prev_k = jnp.concatenate([jnp.where(top < k, pltpu.roll(prev, k, 0), xk[0:SUBLANES]), xk[SUBLANES:]],
                                     axis=0)
            xc = xc + wc[CONV_B - 1 - k:CONV_B - k] * prev_k
        xp_ref[...] = x[tt - SUBLANES:tt]

    xcb = xc.astype(BF16)
    r = jax.nn.sigmoid(_dot(xcb, wr_ref[...]) + br_ref[...])
    ig = jax.nn.sigmoid(_dot(xcb, wi_ref[...]) + bi_ref[...])
    nl = -lam_ref[...]
    sp_lam = jnp.maximum(nl, 0.0) + jnp.log1p(jnp.exp(-jnp.abs(nl)))
    log_a = -C_LRU * r * sp_lam
    a = jnp.exp(log_a)
    u = jnp.sqrt((1.0 + a * a) * jnp.tanh(-log_a)) * (ig * xc)

    grp = (tt // SUBLANES, SUBLANES, W_B)
    a, u = a.reshape(grp), u.reshape(grp)
    r8g = lax.broadcasted_iota(jnp.int32, grp, 1)
    for d in (1, 2, 4):
        keep = r8g >= d
        a_sh = jnp.where(keep, pltpu.roll(a, d, 1), 1.0)
        u_sh = jnp.where(keep, pltpu.roll(u, d, 1), 0.0)
        u = a * u_sh + u
        a = a * a_sh
    a, u = a.reshape(tt, W_B), u.reshape(tt, W_B)
    if short_seq:
        h = a * h0_ref[0] + u
        hl_ref[0] = h
    else:
        h_in = hc_ref[...]
        parts = []
        for g in range(tt // SUBLANES):
            sl = slice(g * SUBLANES, (g + 1) * SUBLANES)
            h_g = a[sl] * h_in + u[sl]
            h_in = h_g[SUBLANES - 1:SUBLANES]
            parts.append(h_g)
        h = jnp.concatenate(parts, axis=0)
        hc_ref[...] = h_in
        hl_ref[0] = parts[-1]
    ob = (h * jax.nn.gelu(gb_ref[0])).astype(BF16)
    y_ref[0] = (res_ref[0] + _dot(oa_ref[0], wo_ref[0:W_A, :]) + _dot(ob, wo_ref[W_A:W_A + W_B, :]))


def _lru_out(xb, gb, h0, cbuf, res, oa, wc, bc, wr, br, wi, bi, lam, wo, tt, short_seq):
    b, t, _ = xb.shape
    seq = lambda bi, ti: (bi, ti, 0)
    per_b = lambda bi, ti: (bi, 0, 0)
    fixed = lambda bi, ti: (0, 0)
    if short_seq:
        assert t == tt
        h0_spec, cb_spec = pl.BlockSpec((1, tt, W_B), seq), pl.BlockSpec((1, tt, W_B), seq)
        hl_spec, hl_shape = pl.BlockSpec((1, tt, W_B), seq), jax.ShapeDtypeStruct((b, t, W_B), F32)
    else:
        h0_spec, cb_spec = pl.BlockSpec((1, 1, W_B), per_b), pl.BlockSpec((1, SUBLANES, W_B), per_b)
        hl_spec, hl_shape = pl.BlockSpec((1, SUBLANES, W_B), per_b), jax.ShapeDtypeStruct((b, SUBLANES, W_B), F32)
    return pl.pallas_call(
        functools.partial(_lru_kernel, tt=tt, short_seq=short_seq),
        grid=(b, t // tt),
        in_specs=[pl.BlockSpec((1, tt, W_B), seq), pl.BlockSpec((1, tt, W_B), seq), h0_spec, cb_spec,
                  pl.BlockSpec((1, tt, D_MODEL), seq), pl.BlockSpec((1, tt, W_A), seq),
                  pl.BlockSpec((CONV_B, W_B), fixed), pl.BlockSpec((1, W_B), fixed),
                  pl.BlockSpec((W_B, W_B), fixed), pl.BlockSpec((1, W_B), fixed),
                  pl.BlockSpec((W_B, W_B), fixed), pl.BlockSpec((1, W_B), fixed),
                  pl.BlockSpec((1, W_B), fixed), pl.BlockSpec((W_A + W_B, D_MODEL), fixed)],
        out_specs=[pl.BlockSpec((1, tt, D_MODEL), seq), hl_spec],
        out_shape=[jax.ShapeDtypeStruct((b, t, D_MODEL), F32), hl_shape],
        scratch_shapes=[pltpu.VMEM((SUBLANES, W_B), F32), pltpu.VMEM((1, W_B), F32)],
        compiler_params=_cparams(("parallel", "arbitrary")),
        name="rglru_out",
    )(xb, gb, h0, cbuf, res, oa, wc, bc, wr, br, wi, bi, lam, wo)


def _odd_kernel(x_ref, g_ref, win_ref, gv_ref, wmix_ref, bmix_ref, wout_ref, y_ref, vn_ref, um_ref,
                *, tm, rblk):
    x = x_ref[...]
    xn = _rms(x, g_ref[...]).astype(BF16)
    z = jax.nn.gelu(_dot(xn, win_ref[...]))
    for h in range(H_C):
        lo, hi = h * CW_C, (h + 1) * CW_C
        vh = z[:, W_C + lo:W_C + hi]
        ms = jnp.mean(vh * vh, axis=-1, keepdims=True)
        vnh = vh * lax.rsqrt(ms + EPS) * gv_ref[:, lo:hi]
        vn_ref[:, lo:hi] = vnh
        if rblk == SUBLANES:
            v3 = vnh.reshape(tm // SUBLANES, SUBLANES, CW_C)
            mixed = wmix_ref[0, :, lo:hi][None] * v3 + bmix_ref[:, lo:hi][None]
            for kk in range(1, SUBLANES):
                mixed = mixed + wmix_ref[kk, :, lo:hi][None] * pltpu.roll(v3, kk, 1)
            um_ref[:, lo:hi] = (z[:, lo:hi] * mixed.reshape(tm, CW_C)).astype(BF16)
        else:
            vnb = vnh.astype(BF16)
            wm = wmix_ref[h]
            for c in range(0, tm // rblk, 2):
                r0, r1, r2 = c * rblk, (c + 1) * rblk, (c + 2) * rblk
                mixed2 = _dot(wm, jnp.concatenate([vnb[r0:r1], vnb[r1:r2]], axis=1))
                for ra, rb, mixed in ((r0, r1, mixed2[:, :CW_C]), (r1, r2, mixed2[:, CW_C:])):
                    um_ref[ra:rb, lo:hi] = (z[ra:rb, lo:hi] * (mixed + bmix_ref[:, lo:hi])).astype(BF16)
    y_ref[...] = x + _dot(um_ref[...], wout_ref[...])


def _odd_mixer(x, g, win, gv, wmix, bmix, wout, tm, rblk):
    m = x.shape[0]
    row = lambda i: (i, 0)
    fixed = lambda i: (0, 0)
    return pl.pallas_call(
        functools.partial(_odd_kernel, tm=tm, rblk=rblk),
        grid=(m // tm,),
        in_specs=[pl.BlockSpec((tm, D_MODEL), row), pl.BlockSpec((1, D_MODEL), fixed),
                  pl.BlockSpec((D_MODEL, 2 * W_C), fixed), pl.BlockSpec((1, W_C), fixed),
                  pl.BlockSpec(wmix.shape, lambda i: (0, 0, 0)),
                  pl.BlockSpec((rblk, W_C), fixed), pl.BlockSpec((W_C, D_MODEL), fixed)],
        out_specs=[pl.BlockSpec((tm, D_MODEL), row), pl.BlockSpec((tm, W_C), row)],
        out_shape=[jax.ShapeDtypeStruct((m, D_MODEL), F32), jax.ShapeDtypeStruct((m, W_C), F32)],
        scratch_shapes=[pltpu.VMEM((tm, W_C), BF16)],
        compiler_params=_cparams(("parallel",)),
        name="odd_mixer",
    )(x, g, win, gv, wmix, bmix, wout)


def _ffn_kernel(x_ref, g_ref, init_ref, wg_ref, wu_ref, wc_ref, bc_ref, wd_ref,
                y_ref, gt_ref, xn_ref, car_ref, *, tm, tiles_per_seq, short_seq):
    m = pl.program_id(0)
    j = pl.program_id(1)

    @pl.when(j == 0)
    def _():
        x = x_ref[...]
        xn_ref[...] = _rms(x, g_ref[...]).astype(BF16)
        y_ref[...] = x

    if not short_seq:
        @pl.when(m % tiles_per_seq == 0)
        def _():
            car_ref[j] = init_ref[0]

    xn = xn_ref[...]
    g = _dot(xn, wg_ref[...])
    u = _dot(xn, wu_ref[...])
    g1 = pltpu.roll(g, 1, 0)
    g2 = pltpu.roll(g, 2, 0)
    if short_seq:
        hist = init_ref[...]
        r8 = lax.broadcasted_iota(jnp.int32, g.shape, 0) % SUBLANES
        p1 = jnp.where(r8 == 0, pltpu.roll(hist, tm - 1, 0), g1)
        p2 = jnp.where(r8 < 2, hist, g2)
        gt_ref[...] = g
    else:
        prev = car_ref[j]
        r8 = lax.broadcasted_iota(jnp.int32, prev.shape, 0)
        p1 = jnp.concatenate([jnp.where(r8 < 1, pltpu.roll(prev, 1, 0), g1[0:SUBLANES]), g1[SUBLANES:]], axis=0)
        p2 = jnp.concatenate([jnp.where(r8 < 2, pltpu.roll(prev, 2, 0), g2[0:SUBLANES]), g2[SUBLANES:]], axis=0)
        tail = g[tm - SUBLANES:tm]
        car_ref[j] = tail
        gt_ref[0] = tail
    wc = wc_ref[...]
    gc = bc_ref[...] + wc[2:3] * g + wc[1:2] * p1 + wc[0:1] * p2
    hid = (jax.nn.gelu(gc) * u).astype(BF16)
    y_ref[...] += _dot(hid, wd_ref[...])


def _ffn(x, g, init, wg, wu, wc, bc, wd, layer, tm, tf, seq_len):
    m = x.shape[0]
    nff = D_FF // tf
    short_seq = seq_len < tm
    if short_seq:
        assert seq_len == SUBLANES and m == tm
        tiles_per_seq = 1
        init_spec = pl.BlockSpec((tm, tf), lambda i, j: (i, j))
        gt_spec = pl.BlockSpec((tm, tf), lambda i, j: (i, j))
        gt_shape = jax.ShapeDtypeStruct((m, D_FF), F32)
    else:
        assert seq_len % tm == 0
        tiles_per_seq = seq_len // tm
        init_spec = pl.BlockSpec((1, SUBLANES, tf), lambda i, j: (i // tiles_per_seq, 0, j))
        gt_spec = pl.BlockSpec((1, SUBLANES, tf), lambda i, j: (i, 0, j))
        gt_shape = jax.ShapeDtypeStruct((m // tm, SUBLANES, D_FF), F32)
    return pl.pallas_call(
        functools.partial(_ffn_kernel, tm=tm, tiles_per_seq=tiles_per_seq, short_seq=short_seq),
        grid=(m // tm, nff),
        in_specs=[pl.BlockSpec((tm, D_MODEL), lambda i, j: (i, 0)),
                  pl.BlockSpec((1, D_MODEL), lambda i, j: (0, 0)),
                  init_spec,
                  pl.BlockSpec((None, D_MODEL, tf), lambda i, j: (layer, 0, j)),
                  pl.BlockSpec((None, D_MODEL, tf), lambda i, j: (layer, 0, j)),
                  pl.BlockSpec((None, CONV_F, tf), lambda i, j: (layer, 0, j)),
                  pl.BlockSpec((None, 1, tf), lambda i, j: (layer, 0, j)),
                  pl.BlockSpec((None, tf, D_MODEL), lambda i, j: (layer, j, 0))],
        out_specs=[pl.BlockSpec((tm, D_MODEL), lambda i, j: (i, 0)), gt_spec],
        out_shape=[jax.ShapeDtypeStruct((m, D_MODEL), F32), gt_shape],
        scratch_shapes=[pltpu.VMEM((tm, D_MODEL), BF16), pltpu.VMEM((nff, SUBLANES, tf), F32)],
        compiler_params=_cparams(("arbitrary", "arbitrary")),
        name="conv_ffn",
    )(x, g, init, wg, wu, wc, bc, wd)


def _block_diag(w):
    n, c, d = w.shape
    eye = jnp.eye(n, dtype=w.dtype)
    return (eye[:, None, :, None] * w[:, :, None, :]).reshape(n * c, n * d)


def _suffix_tri(n):
    j = np.arange(n)[:, None]
    s = np.arange(n)[None, :]
    return jnp.asarray((j >= s).astype(np.float32), dtype=BF16)


def kernel(x_prompt, x_sample, cache_k, cache_v, state_lru_h, state_lru_conv, state_ffn_conv, page_table,
           g_mix, g_ffn, w_in_even, g_q, g_k, sb_bias, w_conv_lru, b_conv_lru, w_rgate, b_rgate, w_igate,
           b_igate, lru_lambda, w_out_even, w_in_odd, g_v, w_spatial, b_spatial, w_out_odd,
           w_gate, w_up, w_ffn_conv, b_ffn_conv, w_down):
    bp, tp, _ = x_prompt.shape
    bs, ts, _ = x_sample.shape
    depth = g_mix.shape[0]
    n_pool = cache_k.shape[1]
    mp, ms = bp * tp, bs * ts
    assert ts == SUBLANES and tp % 512 == 0

    tm_p, tm_s = 512, ms
    tm_f = 1024
    tk, nsub, unroll, chunk = 256, 8, 8, 1024
    tt_p = 256
    tf = 1024
    n_group = 32

    yp = x_prompt.reshape(mp, D_MODEL)
    ys = x_sample.reshape(ms, D_MODEL)

    head_mean = jnp.asarray(np.kron(np.eye(H_A), np.full((HD_A, HD_A), 1.0 / HD_A)), dtype=BF16)
    tri = _suffix_tri(tk)
    tri_s = _suffix_tri(2 * PAGE)
    ck = jnp.transpose(cache_k, (0, 1, 3, 4, 2)).reshape(cache_k.shape[0], n_pool, W_A, PAGE)
    cv = jnp.transpose(cache_v, (0, 1, 3, 4, 2)).reshape(cache_v.shape[0], n_pool, W_A, PAGE)
    row_head = np.arange(H_A * SUBLANES)[:, None] // SUBLANES
    lane_head = np.arange(W_A)[None, :] // HD_A
    qbd_mask = jnp.asarray(row_head == lane_head)

    ffn_w = (w_gate.astype(BF16), w_up.astype(BF16), w_ffn_conv, b_ffn_conv.reshape(depth, 1, D_FF),
             w_down.astype(BF16))
    n_even = (depth + 1) // 2
    kv_stack = None
    outs = {n: [] for n in ("ks", "vs", "hp", "hs", "cp", "cs", "chp", "chs", "fp", "fs")}
    last_chunk_start = ((tp - 1) // CHUNK) * CHUNK

    for l in range(depth):
        gm = g_mix[l].reshape(1, D_MODEL)
        if l % 2 == 0:
            e = l // 2
            w_in = w_in_even[e].astype(BF16)
            gq = (jnp.tile(g_q[e], H_A) * (HD_A ** -0.5 * LOG2E)).reshape(1, W_A)
            gk = jnp.tile(g_k[e], H_A).reshape(1, W_A)
            bias2 = sb_bias[e] * LOG2E
            bias_hi = bias2.astype(BF16).astype(F32)
            bias_hl = jnp.stack([bias_hi, (bias2 - bias_hi).astype(BF16).astype(F32)])
            wr = _block_diag(w_rgate[e]).astype(BF16)
            wi = _block_diag(w_igate[e]).astype(BF16)
            br = b_rgate[e].reshape(1, W_B)
            bi = b_igate[e].reshape(1, W_B)
            lam = lru_lambda[e].reshape(1, W_B)
            bc = b_conv_lru[e].reshape(1, W_B)
            w_out = w_out_even[e].astype(BF16)
            lru_w = (w_conv_lru[e], bc, wr, br, wi, bi, lam)

            q, kt, vt, kb, vb, xb, gb = _even_in(yp, gm, w_in, head_mean, gq, gk, tm_p, tp, e, n_even, kv_stack)
            kv_stack = (kt, vt)
            oa = _attn_prompt(q.reshape(bp, tp, W_A), kb.reshape(bp, tp, W_A), vb.reshape(bp, tp, W_A),
                              bias_hl, tri, tk, nsub, unroll, chunk)
            xb3 = xb.reshape(bp, tp, W_B)
            y3, hl = _lru_out(xb3, gb.reshape(bp, tp, W_B), jnp.zeros((bp, 1, W_B), F32),
                              jnp.zeros((bp, SUBLANES, W_B), F32), yp.reshape(bp, tp, D_MODEL), oa,
                              *lru_w, w_out, tt_p, False)
            yp = y3.reshape(mp, D_MODEL)
            outs["hp"].append(hl[:, SUBLANES - 1])
            outs["cp"].append(xb3[:, tp - (CONV_B - 1):])

            q, k, v, kb, vb, xb, gb = _even_in(ys, gm, w_in, head_mean, gq, gk, tm_s)
            qbd = jnp.where(qbd_mask, jnp.tile(q.reshape(bs, ts, W_A), (1, H_A, 1)), jnp.zeros((), BF16))
            brow = jnp.repeat(bias2, SUBLANES).reshape(H_A * SUBLANES, 1)
            pad = ((0, 0), (0, 0), (0, PAGE - ts))
            knew = jnp.pad(jnp.swapaxes(kb.reshape(bs, ts, W_A), 1, 2), pad)
            vnew = jnp.pad(jnp.swapaxes(vb.reshape(bs, ts, W_A), 1, 2), pad)
            oa = _attn_sample(page_table, qbd, brow, knew, vnew, tri_s, ck, cv, e, n_group)
            xb3 = xb.reshape(bs, ts, W_B)
            hist = jnp.pad(state_lru_conv[e], ((0, 0), (0, SUBLANES - (CONV_B - 1)), (0, 0))).reshape(1, ms, W_B)
            h0 = jnp.repeat(state_lru_h[e], ts, axis=0).reshape(1, ms, W_B)
            y3, hl = _lru_out(xb.reshape(1, ms, W_B), gb.reshape(1, ms, W_B), h0, hist,
                              ys.reshape(1, ms, D_MODEL), oa.reshape(1, ms, W_A), *lru_w, w_out, ms, True)
            ys = y3.reshape(ms, D_MODEL)
            outs["ks"].append(k.reshape(bs, ts, H_A, HD_A))
            outs["vs"].append(v.reshape(bs, ts, H_A, HD_A))
            outs["hs"].append(hl.reshape(bs, ts, W_B)[:, ts - 1])
            outs["cs"].append(xb3[:, ts - (CONV_B - 1):])
        else:
            o = l // 2
            w_in = w_in_odd[o].astype(BF16)
            gv = g_v[o].reshape(1, W_C)
            w_out = w_out_odd[o].astype(BF16)
            w_tril = jnp.tril(w_spatial[o])
            wmix_p = w_tril.astype(BF16)
            bmix_p = jnp.repeat(b_spatial[o].T, CW_C, axis=1)
            taps = jnp.stack([jnp.pad(jnp.diagonal(w_tril[:, :ts, :ts], offset=-k, axis1=1, axis2=2),
                                      ((0, 0), (k, 0))) for k in range(ts)])
            wmix_s = jnp.repeat(jnp.transpose(taps, (0, 2, 1)), CW_C, axis=2)
            bmix_s = bmix_p[:ts]

            yp, vn = _odd_mixer(yp, gm, w_in, gv, wmix_p, bmix_p, w_out, tm_p, CHUNK)
            outs["chp"].append(vn.reshape(bp, tp, W_C)[:, last_chunk_start:])
            ys, vn = _odd_mixer(ys, gm, w_in, gv, wmix_s, bmix_s, w_out, tm_s, ts)
            outs["chs"].append(vn.reshape(bs, ts, W_C))

        gf = g_ffn[l].reshape(1, D_MODEL)
        yp, gt = _ffn(yp, gf, jnp.zeros((bp, SUBLANES, D_FF), F32), *ffn_w, l, tm_f, tf, tp)
        tiles = tp // tm_f
        outs["fp"].append(gt.reshape(bp, tiles, SUBLANES, D_FF)[:, tiles - 1, SUBLANES - (CONV_F - 1):])
        hist = jnp.pad(state_ffn_conv[l], ((0, 0), (0, SUBLANES - (CONV_F - 1)), (0, 0))).reshape(ms, D_FF)
        ys, gt = _ffn(ys, gf, hist, *ffn_w, l, tm_s, tf, ts)
        outs["fs"].append(gt.reshape(bs, ts, D_FF)[:, ts - (CONV_F - 1):])

    st = lambda n: jnp.stack(outs[n])
    to_thd = lambda a: jnp.transpose(a.reshape(n_even, bp, H_A, HD_A, tp), (0, 1, 4, 2, 3))
    return (yp.reshape(bp, tp, D_MODEL), ys.reshape(bs, ts, D_MODEL),
            to_thd(kv_stack[0]), to_thd(kv_stack[1]), st("ks"), st("vs"), st("hp"), st("hs"), st("cp"), st("cs"),
            st("chp"), st("chs"), st("fp"), st("fs"))
```
